```python
import math
import jax, jax.numpy as jnp
from jax import lax
import numpy as np

D_MODEL = 1024
BATCH = 8
SEQ = 4096
DEPTH = 4

NORM_EPS = 1e-6
DN_HEAD_DIM = 128
DN_WIDTH = D_MODEL // 2
DN_HEADS = DN_WIDTH // DN_HEAD_DIM
DN_CHUNK = 64
DN_CONV = 4
GM_GROUP_DIM = 64
GM_WIDTH = D_MODEL // 4
GM_GROUPS = GM_WIDTH // GM_GROUP_DIM
GM_CHUNK = 128
SW_HEAD_DIM = 64
SW_WIDTH = D_MODEL // 4
SW_HEADS = SW_WIDTH // SW_HEAD_DIM
SW_PATTERNS = ((128, 1), (512, 4), (2048, 16))
SW_BLOCK = 128
ROPE_THETA = 500000.0
ROPE_DIM = SW_HEAD_DIM // 4
MIX_WIDTH = DN_WIDTH + GM_WIDTH + SW_WIDTH
IN_SPLITS = (3 * DN_WIDTH, 4 * DN_WIDTH, 4 * DN_WIDTH + DN_HEADS, 4 * DN_WIDTH + 2 * DN_HEADS,
             4 * DN_WIDTH + 2 * DN_HEADS + 2 * GM_WIDTH)
IN_WIDTH = IN_SPLITS[-1] + len(SW_PATTERNS) * 3 * SW_WIDTH
FFN_HIDDEN = -(-8 * D_MODEL // (3 * 256)) * 256

kernel_name = "hybrid_parallel_heads_decoder"


def rms_norm(x, w):
    xf = x.astype(jnp.float32)
    y = xf * lax.rsqrt(jnp.mean(xf * xf, axis=-1, keepdims=True) + NORM_EPS)
    return (y * w.astype(jnp.float32)).astype(x.dtype)


def layer_norm(x, g, b):
    xf = x.astype(jnp.float32)
    mu = jnp.mean(xf, axis=-1, keepdims=True)
    xc = xf - mu
    var = jnp.mean(xc * xc, axis=-1, keepdims=True)
    return (xc * lax.rsqrt(var + NORM_EPS) * g.astype(jnp.float32) + b.astype(jnp.float32)).astype(x.dtype)


def l2_norm(x):
    return x * lax.rsqrt(jnp.sum(x * x, axis=-1, keepdims=True) + NORM_EPS)


def rotary_tables(seq):
    inv = ROPE_THETA ** (-jnp.arange(0, ROPE_DIM, 2, dtype=jnp.float32) / ROPE_DIM)
    ang = jnp.arange(seq, dtype=jnp.float32)[:, None] * inv[None, :]
    return jnp.cos(ang), jnp.sin(ang)


def apply_partial_rotary(x, cos, sin):
    half = ROPE_DIM // 2
    x1, x2, xp = x[..., :half], x[..., half:ROPE_DIM], x[..., ROPE_DIM:]
    c = cos[None, :, None, :]
    s = sin[None, :, None, :]
    return jnp.concatenate([x1 * c - x2 * s, x2 * c + x1 * s, xp], axis=-1)


def causal_dwconv_silu(x, w):
    k, ch = w.shape
    y = lax.conv_general_dilated(x, w[:, None, :].astype(x.dtype), window_strides=(1,),
                                 padding=[(k - 1, 0)], dimension_numbers=('NWC', 'WIO', 'NWC'),
                                 feature_group_count=ch)
    return jax.nn.silu(y)


def gated_delta_rule(q, k, v, g, beta):
    bsz, h, t, dk = q.shape
    dv = v.shape[-1]
    n = t // DN_CHUNK
    q = q.reshape(bsz, h, n, DN_CHUNK, dk) * (dk ** -0.5)
    k = k.reshape(bsz, h, n, DN_CHUNK, dk)
    v = v.reshape(bsz, h, n, DN_CHUNK, dv)
    beta = beta.reshape(bsz, h, n, DN_CHUNK)
    gcum = jnp.cumsum(g.reshape(bsz, h, n, DN_CHUNK), axis=-1)
    idx = jnp.arange(DN_CHUNK)
    causal = idx[:, None] >= idx[None, :]
    strict = idx[:, None] > idx[None, :]
    decay = jnp.exp(jnp.where(causal, gcum[..., :, None] - gcum[..., None, :], -jnp.inf))
    kb = k * beta[..., None]
    a_low = jnp.where(strict, jnp.einsum('bhnid,bhnjd->bhnij', kb, k) * decay, 0.0)
    eye = jnp.eye(DN_CHUNK, dtype=q.dtype)
    rhs = jnp.concatenate([v * beta[..., None], kb * jnp.exp(gcum)[..., None]], axis=-1)
    sol = lax.linalg.triangular_solve(a_low + eye, rhs, left_side=True, lower=True)
    u, w = sol[..., :dv], sol[..., dv:]
    a_qk = jnp.einsum('bhnid,bhnjd->bhnij', q, k) * decay
    q_dec = q * jnp.exp(gcum)[..., None]
    g_last = gcum[..., -1]
    k_dec = k * jnp.exp(g_last[..., None] - gcum)[..., None]

    def step(state, xs):
        u_c, w_c, qd_c, a_c, kd_c, gl_c = xs
        v_new = u_c - jnp.einsum('bhcd,bhde->bhce', w_c, state)
        o = jnp.einsum('bhcd,bhde->bhce', qd_c, state) + jnp.einsum('bhij,bhje->bhie', a_c, v_new)
        state = state * jnp.exp(gl_c)[..., None, None] + jnp.einsum('bhcd,bhce->bhde', kd_c, v_new)
        return state, o

    xs = (jnp.moveaxis(u, 2, 0), jnp.moveaxis(w, 2, 0), jnp.moveaxis(q_dec, 2, 0),
          jnp.moveaxis(a_qk, 2, 0), jnp.moveaxis(k_dec, 2, 0), jnp.moveaxis(g_last, 2, 0))
    s0 = jnp.zeros((bsz, h, dk, dv), q.dtype)
    _, o = lax.scan(step, s0, xs)
    return jnp.moveaxis(o, 0, 2).reshape(bsz, h, t, dv)


def deltanet_mixer(qkv, z, a, b, conv_w, a_log, dt_bias, out_norm_w):
    bsz, t, _ = qkv.shape
    qkv = causal_dwconv_silu(qkv, conv_w).astype(jnp.float32)
    q, k, v = jnp.split(qkv, 3, axis=-1)
    heads = lambda y: y.reshape(bsz, t, DN_HEADS, DN_HEAD_DIM).transpose(0, 2, 1, 3)
    q, k, v = l2_norm(heads(q)), l2_norm(heads(k)), heads(v)
    g = -jnp.exp(a_log.astype(jnp.float32)) * jax.nn.softplus(a.astype(jnp.float32) + dt_bias.astype(jnp.float32))
    beta = jax.nn.sigmoid(b.astype(jnp.float32))
    o = gated_delta_rule(q, k, v, g.transpose(0, 2, 1), beta.transpose(0, 2, 1))
    o = o.transpose(0, 2, 1, 3)
    zg = z.astype(jnp.float32).reshape(bsz, t, DN_HEADS, DN_HEAD_DIM)
    o = rms_norm(o, out_norm_w) * jax.nn.silu(zg)
    return o.reshape(bsz, t, DN_WIDTH)


def spatial_gating_mixer(uv, ln_g, ln_b, w_s, b_s):
    bsz, t, _ = uv.shape
    zz = jax.nn.gelu(uv.astype(jnp.float32), approximate=False)
    u, v = jnp.split(zz, 2, axis=-1)
    v = layer_norm(v, ln_g, ln_b)
    n = t // GM_CHUNK
    v = v.reshape(bsz, n, GM_CHUNK, GM_GROUPS, GM_GROUP_DIM)
    causal = jnp.tril(jnp.ones((GM_CHUNK, GM_CHUNK), dtype=bool))
    ws = jnp.where(causal, w_s.astype(jnp.float32), 0.0)
    sv = jnp.einsum('gij,bnjgc->bnigc', ws, v) + b_s.astype(jnp.float32).T[None, None, :, :, None]
    return u * sv.reshape(bsz, t, GM_WIDTH)


def dilated_window_attention(q, k, v, dilation, span):
    bsz, t, h, d = q.shape
    length = t // dilation
    nb = -(-length // SW_BLOCK)
    lp = nb * SW_BLOCK

    def to_sub(y):
        y = y.reshape(bsz, length, dilation, h, d).transpose(0, 2, 3, 1, 4)
        y = jnp.pad(y, ((0, 0), (0, 0), (0, 0), (0, lp - length), (0, 0)))
        return y.reshape(bsz, dilation, h, nb, SW_BLOCK, d)

    def with_prev(y):
        prev = jnp.pad(y, ((0, 0), (0, 0), (0, 0), (1, 0), (0, 0), (0, 0)))[:, :, :, :-1]
        return jnp.concatenate([prev, y], axis=-2)

    qs = to_sub(q)
    kk, vv = with_prev(to_sub(k)), with_prev(to_sub(v))
    s = jnp.einsum('brhnie,brhnje->brhnij', qs, kk) * (d ** -0.5)
    blk = jnp.arange(nb)[:, None, None] * SW_BLOCK
    qpos = blk + jnp.arange(SW_BLOCK)[None, :, None]
    kpos = blk - SW_BLOCK + jnp.arange(2 * SW_BLOCK)[None, None, :]
    dist = qpos - kpos
    valid = (dist >= 0) & (dist <= span) & (kpos >= 0)
    s = jnp.where(valid, s, -jnp.inf)
    m = jnp.max(s, axis=-1, keepdims=True)
    p = jnp.exp(s - m)
    l = jnp.sum(p, axis=-1, keepdims=True)
    o = jnp.einsum('brhnij,brhnje->brhnie', p, vv) / l
    lse = (m + jnp.log(l))[..., 0]
    o = o.reshape(bsz, dilation, h, lp, d)[:, :, :, :length]
    o = o.transpose(0, 3, 1, 2, 4).reshape(bsz, t, h, d)
    lse = lse.reshape(bsz, dilation, h, lp)[:, :, :, :length]
    lse = lse.transpose(0, 3, 1, 2).reshape(bsz, t, h)
    return o, lse


def dilated_attention_mixer(qkv, q_norm_w, k_norm_w, cos, sin):
    bsz, t, _ = qkv.shape
    parts = qkv.astype(jnp.float32).reshape(bsz, t, len(SW_PATTERNS), 3, SW_HEADS, SW_HEAD_DIM)
    outs, lses = [], []
    for gi, (window, dilation) in enumerate(SW_PATTERNS):
        q = apply_partial_rotary(rms_norm(parts[:, :, gi, 0], q_norm_w), cos, sin)
        k = apply_partial_rotary(rms_norm(parts[:, :, gi, 1], k_norm_w), cos, sin)
        o, lse = dilated_window_attention(q, k, parts[:, :, gi, 2], dilation, window // dilation)
        outs.append(o)
        lses.append(lse)
    o = jnp.stack(outs, axis=0)
    wts = jax.nn.softmax(jnp.stack(lses, axis=0), axis=0)
    return jnp.sum(wts[..., None] * o, axis=0).reshape(bsz, t, SW_WIDTH)


def setup_inputs(seed: int = 0) -> dict:
    key = jax.random.key(seed)
    ks = jax.random.split(key, 20)
    f32 = jnp.float32
    nl = DEPTH

    def nrm(k, shape, scale):
        return jax.random.normal(k, shape, f32) * scale

    dt = jnp.exp(jax.random.uniform(ks[10], (nl, DN_HEADS), f32, math.log(1e-3), math.log(1e-1)))
    return {
        'x': nrm(ks[0], (BATCH, SEQ, D_MODEL), 1.0),
        'c': nrm(ks[1], (BATCH, D_MODEL), 1.0),
        'w_mod': nrm(ks[2], (nl, D_MODEL, 6 * D_MODEL), 0.5 * D_MODEL ** -0.5),
        'b_mod': nrm(ks[3], (nl, 6 * D_MODEL), 0.01),
        'mix_norm_w': 1.0 + nrm(ks[4], (nl, D_MODEL), 0.02),
        'ffn_norm_w': 1.0 + nrm(ks[5], (nl, D_MODEL), 0.02),
        'w_in': nrm(ks[6], (nl, D_MODEL, IN_WIDTH), D_MODEL ** -0.5),
        'w_out': nrm(ks[7], (nl, MIX_WIDTH, D_MODEL), MIX_WIDTH ** -0.5),
        'dn_conv_w': nrm(ks[8], (nl, DN_CONV, 3 * DN_WIDTH), DN_CONV ** -0.5),
        'dn_a_log': jnp.log(jax.random.uniform(ks[9], (nl, DN_HEADS), f32, 1.0, 16.0)),
        'dn_dt_bias': dt + jnp.log(-jnp.expm1(-dt)),
        'dn_out_norm_w': 1.0 + nrm(ks[11], (nl, DN_HEAD_DIM), 0.02),
        'gm_ln_g': 1.0 + nrm(ks[12], (nl, GM_WIDTH), 0.02),
        'gm_ln_b': nrm(ks[13], (nl, GM_WIDTH), 0.02),
        'gm_w_s': nrm(ks[14], (nl, GM_GROUPS, GM_CHUNK, GM_CHUNK), GM_CHUNK ** -0.5),
        'gm_b_s': 1.0 + nrm(ks[15], (nl, GM_GROUPS, GM_CHUNK), 0.01),
        'sw_q_norm_w': 1.0 + nrm(ks[16], (nl, SW_HEAD_DIM), 0.02),
        'sw_k_norm_w': 1.0 + nrm(ks[17], (nl, SW_HEAD_DIM), 0.02),
        'w_ffn_in': nrm(ks[18], (nl, D_MODEL, 2 * FFN_HIDDEN), D_MODEL ** -0.5),
        'w_ffn_out': nrm(ks[19], (nl, FFN_HIDDEN, D_MODEL), FFN_HIDDEN ** -0.5),
    }


def reference(x, c, w_mod, b_mod, mix_norm_w, ffn_norm_w, w_in, w_out, dn_conv_w, dn_a_log,
              dn_dt_bias, dn_out_norm_w, gm_ln_g, gm_ln_b, gm_w_s, gm_b_s, sw_q_norm_w,
              sw_k_norm_w, w_ffn_in, w_ffn_out):
    bsz, t, _ = x.shape
    cos, sin = rotary_tables(t)
    c_act = jax.nn.silu(c)
    for layer in range(DEPTH):
        mod = jnp.einsum('bd,de->be', c_act, w_mod[layer]) + b_mod[layer]
        shift1, scale1, gate1, shift2, scale2, gate2 = [m[:, None, :] for m in jnp.split(mod, 6, axis=-1)]
        h = rms_norm(x, mix_norm_w[layer]) * (1.0 + scale1) + shift1
        proj = jnp.einsum('btd,de->bte', h, w_in[layer])
        dn_qkv, dn_z, dn_a, dn_b, gm_uv, sw_qkv = jnp.split(proj, IN_SPLITS, axis=-1)
        y_a = deltanet_mixer(dn_qkv, dn_z, dn_a, dn_b, dn_conv_w[layer], dn_a_log[layer],
                             dn_dt_bias[layer], dn_out_norm_w[layer])
        y_b = spatial_gating_mixer(gm_uv, gm_ln_g[layer], gm_ln_b[layer], gm_w_s[layer], gm_b_s[layer])
        y_c = dilated_attention_mixer(sw_qkv, sw_q_norm_w[layer], sw_k_norm_w[layer], cos, sin)
        y = jnp.concatenate([y_a, y_b, y_c], axis=-1).astype(x.dtype)
        x = x + gate1 * jnp.einsum('bte,ed->btd', y, w_out[layer])
        h = rms_norm(x, ffn_norm_w[layer]) * (1.0 + scale2) + shift2
        gate, up = jnp.split(jnp.einsum('btd,df->btf', h, w_ffn_in[layer]), 2, axis=-1)
        x = x + gate2 * jnp.einsum('btf,fd->btd', jax.nn.silu(gate) * up, w_ffn_out[layer])
    return x
```

```python
import functools
import math

import jax
import jax.numpy as jnp
from jax import lax
from jax.experimental import pallas as pl
from jax.experimental.pallas import tpu as pltpu

F32 = jnp.float32
BF16 = jnp.bfloat16
HIGHEST = lax.Precision.HIGHEST

NORM_EPS = 1e-6
DN_HEAD_DIM = 128
DN_CONV = 4
DN_CHUNK = 64
GM_GROUP_DIM = 64
GM_CHUNK = 128
SW_HEAD_DIM = 64
SW_PATTERNS = ((128, 1), (512, 4), (2048, 16))
SW_BLOCK = 128
ROPE_THETA = 500000.0
ROPE_DIM = SW_HEAD_DIM // 4
AB_PAD = 128

VMEM_LIMIT_BYTES = 56 * 1024 * 1024


def _params(n_parallel, n_arbitrary=0):
    return pltpu.CompilerParams(
        dimension_semantics=("parallel",) * n_parallel + ("arbitrary",) * n_arbitrary,
        vmem_limit_bytes=VMEM_LIMIT_BYTES,
    )


def _resident(shape):
    zeros = (0,) * len(shape)
    return pl.BlockSpec(shape, lambda *_: zeros, pipeline_mode=pl.Buffered(1))


def _sigmoid(x):
    return 1.0 / (1.0 + jnp.exp(-x))


def _silu(x):
    return x * _sigmoid(x)


def _dot(a, b):
    return jnp.dot(a, b, preferred_element_type=F32)


def _dot_nt(a, b):
    return lax.dot_general(a, b, (((1,), (1,)), ((), ())), preferred_element_type=F32)


def _dot_tn(a, b):
    return lax.dot_general(a, b, (((0,), (0,)), ((), ())), preferred_element_type=F32)


def _dot_hi(a, b):
    return jnp.dot(a, b, preferred_element_type=F32, precision=HIGHEST)


def _mod_kernel(c_ref, w_ref, b_ref, o_ref):
    c = c_ref[...]
    o_ref[...] = _dot_hi(_silu(c), w_ref[...]) + b_ref[...]


def _modulation(c, w_mod, b_mod):
    nl, d, e = w_mod.shape
    bsz = c.shape[0]
    tn = d
    return pl.pallas_call(
        _mod_kernel,
        grid=(nl, e // tn),
        in_specs=[
            pl.BlockSpec((bsz, d), lambda l, j: (0, 0)),
            pl.BlockSpec((None, d, tn), lambda l, j: (l, 0, j)),
            pl.BlockSpec((None, 1, tn), lambda l, j: (l, 0, j)),
        ],
        out_specs=pl.BlockSpec((None, bsz, tn), lambda l, j: (l, 0, j)),
        out_shape=jax.ShapeDtypeStruct((nl, bsz, e), F32),
        compiler_params=_params(2),
        name="modulation",
    )(c, w_mod, b_mod.reshape(nl, 1, e))


def _group_mean_sq(x, ones_bd, group):
    sq = x * x
    hi = sq.astype(BF16)
    lo = (sq - hi.astype(F32)).astype(BF16)
    return (_dot(hi, ones_bd) + _dot(lo, ones_bd)) * (1.0 / group)


def _rotary(x, cos_f, sin_a, sin_b):
    n = x.shape[-1]
    half = ROPE_DIM // 2
    return x * cos_f + pltpu.roll(x, n - half, 1) * sin_a + pltpu.roll(x, half, 1) * sin_b


def _in_kernel(x_ref, shift_ref, scale_ref, nw_ref, w_ref, cos_ref, sina_ref, sinb_ref,
               qnw_ref, knw_ref, bd_ref,
               dn_ref, z_ref, gm_ref, sw1_ref, sw2_ref, sw3_ref, ab_ref, *, widths):
    x = x_ref[...]
    ms = jnp.mean(x * x, axis=-1, keepdims=True)
    h = x * lax.rsqrt(ms + NORM_EPS) * nw_ref[...] * (1.0 + scale_ref[...]) + shift_ref[...]
    hb = h.astype(BF16)
    offs = [0]
    for wd in widths:
        offs.append(offs[-1] + wd)
    plain = ((dn_ref, 0), (z_ref, 1), (gm_ref, 2), (ab_ref, 6))
    for ref, idx in plain:
        ref[...] = _dot(hb, w_ref[:, offs[idx]:offs[idx + 1]])
    sww = widths[3] // 3
    cos_f, sin_a, sin_b = cos_ref[...], sina_ref[...], sinb_ref[...]
    bd = bd_ref[...]
    for ref, idx in ((sw1_ref, 3), (sw2_ref, 4), (sw3_ref, 5)):
        raw = _dot(hb, w_ref[:, offs[idx]:offs[idx + 1]])
        q, k, v = raw[:, :sww], raw[:, sww:2 * sww], raw[:, 2 * sww:]
        qn = q * lax.rsqrt(_group_mean_sq(q, bd, SW_HEAD_DIM) + NORM_EPS) * qnw_ref[...]
        kn = k * lax.rsqrt(_group_mean_sq(k, bd, SW_HEAD_DIM) + NORM_EPS) * knw_ref[...]
        qn = _rotary(qn, cos_f, sin_a, sin_b) * (SW_HEAD_DIM ** -0.5)
        kn = _rotary(kn, cos_f, sin_a, sin_b)
        ref[:, :sww] = qn.astype(BF16)
        ref[:, sww:2 * sww] = kn.astype(BF16)
        ref[:, 2 * sww:] = v.astype(BF16)


def _in_projection(x, mod4, nw, w_cat, tables, qnw, knw, ones_bd, widths, tm):
    bsz, t, d = x.shape
    cos_f, sin_a, sin_b = tables
    sww = widths[3] // 3
    row = lambda wd: pl.BlockSpec((None, tm, wd), lambda b, i: (b, i, 0))
    modspec = lambda k: pl.BlockSpec((None, None, 1, d), lambda b, i: (b, k, 0, 0))
    tabspec = pl.BlockSpec((tm, sww), lambda b, i: (i, 0))
    out_widths = (widths[0], widths[1], widths[2], widths[3], widths[4], widths[5], widths[6])
    out_dtypes = (F32, F32, F32, BF16, BF16, BF16, F32)
    return pl.pallas_call(
        functools.partial(_in_kernel, widths=widths),
        grid=(bsz, t // tm),
        in_specs=[
            row(d), modspec(0), modspec(1), _resident(nw.shape), _resident(w_cat.shape),
            tabspec, tabspec, tabspec, _resident(qnw.shape), _resident(knw.shape),
            _resident(ones_bd.shape),
        ],
        out_specs=[row(wd) for wd in out_widths],
        out_shape=[jax.ShapeDtypeStruct((bsz, t, wd), dt) for wd, dt in zip(out_widths, out_dtypes)],
        compiler_params=_params(2),
        name="in_projection",
    )(x, mod4, mod4, nw, w_cat, cos_f, sin_a, sin_b, qnw, knw, ones_bd)


def _dn_kernel(qkv_ref, z_ref, ab_ref, cw_ref, alog_ref, dtb_ref, onw_ref, o_ref,
               xp_ref, q_s, k_s, v_s, g_s, bt_s, grow_s, s_ref, *, tt, chunk, heads):
    hd = DN_HEAD_DIM
    width = heads * hd
    nchunks = tt // chunk
    pad = 8

    @pl.when(pl.program_id(1) == 0)
    def _():
        xp_ref[0:pad, :] = jnp.zeros((pad, 3 * width), F32)
        s_ref[...] = jnp.zeros_like(s_ref)

    xp_ref[pad:pad + tt, :] = qkv_ref[...]

    for part in range(3):
        for h in range(heads):
            c0 = part * width + h * hd
            cols = slice(c0, c0 + hd)
            y = cw_ref[DN_CONV - 1:DN_CONV, cols] * xp_ref[pad:pad + tt, cols]
            for tap in range(DN_CONV - 1):
                lag = DN_CONV - 1 - tap
                y = y + cw_ref[tap:tap + 1, cols] * xp_ref[pad - lag:pad - lag + tt, cols]
            y = _silu(y)
            hcols = slice(h * hd, (h + 1) * hd)
            if part == 0:
                y = y * lax.rsqrt(jnp.sum(y * y, axis=-1, keepdims=True) + NORM_EPS)
                q_s[:, hcols] = y * (hd ** -0.5)
            elif part == 1:
                k_s[:, hcols] = y * lax.rsqrt(jnp.sum(y * y, axis=-1, keepdims=True) + NORM_EPS)
            else:
                v_s[:, hcols] = y
    xp_ref[0:pad, :] = xp_ref[tt:tt + pad, :]

    ab = ab_ref[...]
    sp = ab + dtb_ref[...]
    softplus = jnp.maximum(sp, 0.0) + jnp.log1p(jnp.exp(-jnp.abs(sp)))
    g = -jnp.exp(alog_ref[...]) * softplus
    pos = lax.broadcasted_iota(jnp.int32, g.shape, 0) % chunk
    step = 1
    while step < chunk:
        g = g + jnp.where(pos >= step, pltpu.roll(g, step, 0), 0.0)
        step *= 2
    g_s[...] = g
    bt_s[...] = _sigmoid(ab)
    g_t = g.T
    for c in range(nchunks):
        grow_s[c] = g_t[0:8, c * chunk:(c + 1) * chunk]

    ii = lax.broadcasted_iota(jnp.int32, (chunk, chunk), 0)
    jj = lax.broadcasted_iota(jnp.int32, (chunk, chunk), 1)
    causal = ii >= jj
    strict = ii > jj
    eye = jnp.where(ii == jj, 1.0, 0.0).astype(F32)
    n_double = int(math.log2(chunk)) - 1

    def body(c, carry):
        r0 = pl.multiple_of(c * chunk, chunk)
        rows = pl.ds(r0, chunk)
        g_all = g_s[rows, :]
        bt_all = bt_s[rows, :]
        grow_all = grow_s[c]
        for h in range(heads):
            hcols = slice(h * hd, (h + 1) * hd)
            gcol = g_all[:, h:h + 1]
            bcol = bt_all[:, heads + h:heads + h + 1]
            grow = grow_all[h:h + 1, :]
            q = q_s[rows, hcols]
            k = k_s[rows, hcols]
            v = v_s[rows, hcols]
            decay = jnp.where(causal, jnp.exp(gcol - grow), 0.0)
            kb = k * bcol
            kbf = k.astype(BF16)
            a_low = jnp.where(strict, _dot_nt(kb.astype(BF16), kbf) * decay, 0.0)
            inv = eye - a_low
            pw = _dot_hi(a_low, a_low)
            for it in range(n_double):
                inv = inv + _dot_hi(inv, pw)
                if it + 1 < n_double:
                    pw = _dot_hi(pw, pw)
            eg = jnp.exp(gcol)
            rhs = jnp.concatenate([v * bcol, kb * eg], axis=1)
            sol = _dot_hi(inv, rhs)
            u, w = sol[:, :hd], sol[:, hd:]
            a_qk = _dot_nt(q.astype(BF16), kbf) * decay
            state = s_ref[h]
            state_b = state.astype(BF16)
            v_new = u - _dot(w.astype(BF16), state_b)
            v_new_b = v_new.astype(BF16)
            o = _dot((q * eg).astype(BF16), state_b) + _dot(a_qk.astype(BF16), v_new_b)
            g_last = gcol[chunk - 1:chunk, :]
            k_dec = k * jnp.exp(g_last - gcol)
            s_ref[h] = state * jnp.exp(g_last) + _dot_tn(k_dec.astype(BF16), v_new_b)
            on = o * lax.rsqrt(jnp.mean(o * o, axis=-1, keepdims=True) + NORM_EPS) * onw_ref[...]
            o_ref[rows, hcols] = on * _silu(z_ref[rows, hcols])
        return carry

    lax.fori_loop(0, nchunks, body, 0)


def _deltanet(qkv, z, ab, conv_w, alog_pad, dtb_pad, onw, tt):
    bsz, t, w3 = qkv.shape
    width = w3 // 3
    heads = width // DN_HEAD_DIM
    chunk = DN_CHUNK
    row = lambda wd: pl.BlockSpec((None, tt, wd), lambda b, i: (b, i, 0))
    return pl.pallas_call(
        functools.partial(_dn_kernel, tt=tt, chunk=chunk, heads=heads),
        grid=(bsz, t // tt),
        in_specs=[row(w3), row(width), row(AB_PAD), _resident(conv_w.shape),
                  _resident(alog_pad.shape), _resident(dtb_pad.shape), _resident(onw.shape)],
        out_specs=row(width),
        out_shape=jax.ShapeDtypeStruct((bsz, t, width), F32),
        scratch_shapes=[
            pltpu.VMEM((tt + 8, w3), F32),
            pltpu.VMEM((tt, width), F32),
            pltpu.VMEM((tt, width), F32),
            pltpu.VMEM((tt, width), F32),
            pltpu.VMEM((tt, AB_PAD), F32),
            pltpu.VMEM((tt, AB_PAD), F32),
            pltpu.VMEM((tt // chunk, 8, chunk), F32),
            pltpu.VMEM((heads, DN_HEAD_DIM, DN_HEAD_DIM), F32),
        ],
        compiler_params=_params(1, 1),
        name="deltanet",
    )(qkv, z, ab, conv_w, alog_pad, dtb_pad, onw)


def _gm_kernel(uv_ref, lng_ref, lnb_ref, ws_ref, bias_ref, o_ref, *, tt, groups):
    width = groups * GM_GROUP_DIM
    x = uv_ref[...]
    zz = 0.5 * x * (1.0 + lax.erf(x * (2.0 ** -0.5)))
    u, v = zz[:, :width], zz[:, width:]
    mu = jnp.mean(v, axis=-1, keepdims=True)
    vc = v - mu
    var = jnp.mean(vc * vc, axis=-1, keepdims=True)
    vn = (vc * lax.rsqrt(var + NORM_EPS) * lng_ref[...] + lnb_ref[...]).astype(BF16)
    ii = lax.broadcasted_iota(jnp.int32, (GM_CHUNK, GM_CHUNK), 0)
    jj = lax.broadcasted_iota(jnp.int32, (GM_CHUNK, GM_CHUNK), 1)
    ws = [jnp.where(ii >= jj, ws_ref[g], 0.0).astype(BF16) for g in range(groups)]
    bias = bias_ref[...]
    for c in range(tt // GM_CHUNK):
        rows = slice(c * GM_CHUNK, (c + 1) * GM_CHUNK)
        parts = [_dot(ws[g], vn[rows, g * GM_GROUP_DIM:(g + 1) * GM_GROUP_DIM])
                 for g in range(groups)]
        o_ref[rows, :] = u[rows, :] * (jnp.concatenate(parts, axis=1) + bias)


def _spatial_gating(uv, ln_g, ln_b, w_s, bias_full, tt):
    bsz, t, w2 = uv.shape
    width = w2 // 2
    groups = width // GM_GROUP_DIM
    return pl.pallas_call(
        functools.partial(_gm_kernel, tt=tt, groups=groups),
        grid=(bsz, t // tt),
        in_specs=[pl.BlockSpec((None, tt, w2), lambda b, i: (b, i, 0)),
                  _resident(ln_g.shape), _resident(ln_b.shape), _resident(w_s.shape),
                  _resident(bias_full.shape)],
        out_specs=pl.BlockSpec((None, tt, width), lambda b, i: (b, i, 0)),
        out_shape=jax.ShapeDtypeStruct((bsz, t, width), F32),
        compiler_params=_params(2),
        name="spatial_gating",
    )(uv, ln_g, ln_b, w_s, bias_full)


def _sw_kernel(q_ref, k_ref, v_ref, kp_ref, vp_ref, o_ref, lse_ref, kk_s, vv_s, *, tq, heads, span):
    blk = SW_BLOCK
    hd = SW_HEAD_DIM
    kk_s[0:blk, :] = kp_ref[...]
    kk_s[blk:blk + tq, :] = k_ref[...]
    vv_s[0:blk, :] = vp_ref[...]
    vv_s[blk:blk + tq, :] = v_ref[...]
    a = lax.broadcasted_iota(jnp.int32, (blk, 2 * blk), 0)
    c = lax.broadcasted_iota(jnp.int32, (blk, 2 * blk), 1)
    dist = blk + a - c
    in_window = (dist >= 0) & (dist <= span)
    nqb = tq // blk
    for qb in range(nqb):
        rows = slice(qb * blk, (qb + 1) * blk)
        q = q_ref[rows, :]
        kk = kk_s[qb * blk:(qb + 2) * blk, :]
        vv = vv_s[qb * blk:(qb + 2) * blk, :]
        has_prev = (pl.program_id(2) * nqb + qb) > 0
        valid = in_window & ((c >= blk) | has_prev)
        outs, lses = [], []
        for h in range(heads):
            hc = slice(h * hd, (h + 1) * hd)
            s = jnp.where(valid, _dot_nt(q[:, hc], kk[:, hc]), -jnp.inf)
            m = jnp.max(s, axis=-1, keepdims=True)
            p = jnp.exp(s - m)
            l = jnp.sum(p, axis=-1, keepdims=True)
            outs.append(_dot(p.astype(BF16), vv[:, hc]) / l)
            lses.append(jnp.broadcast_to(m + jnp.log(l), (blk, hd)))
        o_ref[rows, :] = jnp.concatenate(outs, axis=1)
        lse_ref[rows, :] = jnp.concatenate(lses, axis=1)


def _window_attention(qkv, window, dilation):
    bsz, t, w3 = qkv.shape
    width = w3 // 3
    heads = width // SW_HEAD_DIM
    span = window // dilation
    assert span <= SW_BLOCK
    length = t // dilation
    assert length % SW_BLOCK == 0
    tq = min(length, 4 * SW_BLOCK)
    nb = tq // SW_BLOCK
    view = qkv.reshape(bsz, length, dilation * w3)
    cur = lambda j: pl.BlockSpec((None, tq, width), lambda b, r, i: (b, i, 3 * r + j))
    prev = lambda j: pl.BlockSpec((None, SW_BLOCK, width),
                                  lambda b, r, i: (b, jnp.maximum(i * nb - 1, 0), 3 * r + j))
    out = pl.BlockSpec((None, tq, width), lambda b, r, i: (b, i, r))
    o, lse = pl.pallas_call(
        functools.partial(_sw_kernel, tq=tq, heads=heads, span=span),
        grid=(bsz, dilation, length // tq),
        in_specs=[cur(0), cur(1), cur(2), prev(1), prev(2)],
        out_specs=[out, out],
        out_shape=[jax.ShapeDtypeStruct((bsz, length, dilation * width), F32)] * 2,
        scratch_shapes=[pltpu.VMEM((tq + SW_BLOCK, width), BF16)] * 2,
        compiler_params=_params(3),
        name=f"window_attention_d{dilation}",
    )(view, view, view, view, view)
    return o.reshape(bsz, t, width), lse.reshape(bsz, t, width)


def _out_kernel(x_ref, ya_ref, yb_ref, o1_ref, l1_ref, o2_ref, l2_ref, o3_ref, l3_ref,
                gate1_ref, shift2_ref, scale2_ref, gate2_ref, fnw_ref,
                wout_ref, wfi_ref, wfo_ref, out_ref, *, ffn_chunks):
    l1, l2, l3 = l1_ref[...], l2_ref[...], l3_ref[...]
    m = jnp.maximum(jnp.maximum(l1, l2), l3)
    e1, e2, e3 = jnp.exp(l1 - m), jnp.exp(l2 - m), jnp.exp(l3 - m)
    yc = (e1 * o1_ref[...] + e2 * o2_ref[...] + e3 * o3_ref[...]) / (e1 + e2 + e3)
    y = jnp.concatenate([ya_ref[...].astype(BF16), yb_ref[...].astype(BF16), yc.astype(BF16)],
                        axis=1)
    x1 = x_ref[...] + gate1_ref[...] * _dot(y, wout_ref[...])
    ms = jnp.mean(x1 * x1, axis=-1, keepdims=True)
    h = x1 * lax.rsqrt(ms + NORM_EPS) * fnw_ref[...] * (1.0 + scale2_ref[...]) + shift2_ref[...]
    hb = h.astype(BF16)
    hidden = wfo_ref.shape[0]
    fc = hidden // ffn_chunks
    acc = None
    for j in range(ffn_chunks):
        gate = _dot(hb, wfi_ref[:, j * fc:(j + 1) * fc])
        up = _dot(hb, wfi_ref[:, hidden + j * fc:hidden + (j + 1) * fc])
        part = _dot((_silu(gate) * up).astype(BF16), wfo_ref[j * fc:(j + 1) * fc, :])
        acc = part if acc is None else acc + part
    out_ref[...] = x1 + gate2_ref[...] * acc


def _out_ffn(x, ya, yb, attn, mod4, fnw, wout, wfi, wfo, tm):
    bsz, t, d = x.shape
    row = lambda wd: pl.BlockSpec((None, tm, wd), lambda b, i: (b, i, 0))
    modspec = lambda k: pl.BlockSpec((None, None, 1, d), lambda b, i: (b, k, 0, 0))
    attn_flat = [a for pair in attn for a in pair]
    return pl.pallas_call(
        functools.partial(_out_kernel, ffn_chunks=2),
        grid=(bsz, t // tm),
        in_specs=[row(d), row(ya.shape[-1]), row(yb.shape[-1])]
        + [row(a.shape[-1]) for a in attn_flat]
        + [modspec(2), modspec(3), modspec(4), modspec(5), _resident(fnw.shape),
           _resident(wout.shape), _resident(wfi.shape), _resident(wfo.shape)],
        out_specs=row(d),
        out_shape=jax.ShapeDtypeStruct((bsz, t, d), F32),
        compiler_params=_params(2),
        name="out_ffn",
    )(x, ya, yb, *attn_flat, mod4, mod4, mod4, mod4, fnw, wout, wfi, wfo)


def _rotary_tables(t, heads):
    half = ROPE_DIM // 2
    inv = ROPE_THETA ** (-jnp.arange(0, ROPE_DIM, 2, dtype=F32) / ROPE_DIM)
    ang = jnp.arange(t, dtype=F32)[:, None] * inv[None, :]
    cos, sin = jnp.cos(ang), jnp.sin(ang)
    rest = SW_HEAD_DIM - ROPE_DIM
    ones = jnp.ones((t, rest), F32)
    zeros = jnp.zeros((t, rest), F32)
    zh = jnp.zeros((t, half), F32)
    cos_f = jnp.concatenate([cos, cos, ones], axis=1)
    sin_a = jnp.concatenate([-sin, zh, zeros], axis=1)
    sin_b = jnp.concatenate([zh, sin, zeros], axis=1)
    tile = lambda a: jnp.tile(a, (1, heads))
    return tile(cos_f), tile(sin_a), tile(sin_b)


def kernel(x, c, w_mod, b_mod, mix_norm_w, ffn_norm_w, w_in, w_out, dn_conv_w, dn_a_log,
           dn_dt_bias, dn_out_norm_w, gm_ln_g, gm_ln_b, gm_w_s, gm_b_s, sw_q_norm_w,
           sw_k_norm_w, w_ffn_in, w_ffn_out):
    bsz, t, d = x.shape
    nl = w_mod.shape[0]
    dn_heads = dn_a_log.shape[1]
    dn_width = dn_heads * DN_HEAD_DIM
    gm_groups = gm_w_s.shape[1]
    gm_width = gm_groups * GM_GROUP_DIM
    npat = len(SW_PATTERNS)
    sw_width = (w_in.shape[2] - 4 * dn_width - 2 * dn_heads - 2 * gm_width) // (3 * npat)
    sw_heads = sw_width // SW_HEAD_DIM
    assert 2 * dn_heads <= AB_PAD

    o_z = 3 * dn_width
    o_a = o_z + dn_width
    o_gm = o_a + 2 * dn_heads
    o_sw = o_gm + 2 * gm_width
    widths = (3 * dn_width, dn_width, 2 * gm_width) + (3 * sw_width,) * npat + (AB_PAD,)

    mod = _modulation(c, w_mod, b_mod)
    tables = _rotary_tables(t, sw_heads)
    ii = jnp.arange(sw_width) // SW_HEAD_DIM
    ones_bd = (ii[:, None] == ii[None, :]).astype(BF16)
    tm = 512

    for layer in range(nl):
        wl = w_in[layer]
        ab_cols = jnp.zeros((d, AB_PAD), F32).at[:, :2 * dn_heads].set(wl[:, o_a:o_gm])
        w_cat = jnp.concatenate(
            [wl[:, :o_a], wl[:, o_gm:o_sw], wl[:, o_sw:], ab_cols], axis=1).astype(BF16)
        mod4 = mod[layer].reshape(bsz, 6, 1, d)
        dn_qkv, dn_z, gm_uv, sw1, sw2, sw3, dn_ab = _in_projection(
            x, mod4, mix_norm_w[layer].reshape(1, d), w_cat, tables,
            jnp.tile(sw_q_norm_w[layer], sw_heads).reshape(1, sw_width),
            jnp.tile(sw_k_norm_w[layer], sw_heads).reshape(1, sw_width),
            ones_bd, widths, tm)

        pad_heads = lambda v: jnp.zeros((1, AB_PAD), F32).at[0, :dn_heads].set(v)
        y_a = _deltanet(dn_qkv, dn_z, dn_ab, dn_conv_w[layer], pad_heads(dn_a_log[layer]),
                        pad_heads(dn_dt_bias[layer]), dn_out_norm_w[layer].reshape(1, DN_HEAD_DIM),
                        tt=512)
        bias_full = jnp.repeat(gm_b_s[layer].T, GM_GROUP_DIM, axis=1)
        y_b = _spatial_gating(gm_uv, gm_ln_g[layer].reshape(1, gm_width),
                              gm_ln_b[layer].reshape(1, gm_width), gm_w_s[layer], bias_full,
                              tt=512)
        attn = [_window_attention(sw, window, dilation)
                for sw, (window, dilation) in zip((sw1, sw2, sw3), SW_PATTERNS)]
        x = _out_ffn(x, y_a, y_b, attn, mod4, ffn_norm_w[layer].reshape(1, d),
                     w_out[layer].astype(BF16), w_ffn_in[layer].astype(BF16),
                     w_ffn_out[layer].astype(BF16), tm)
    return x
```

```python
import functools
import math

import jax
import jax.numpy as jnp
from jax import lax
from jax.experimental import pallas as pl
from jax.experimental.pallas import tpu as pltpu

F32 = jnp.float32
BF16 = jnp.bfloat16
HIGHEST = lax.Precision.HIGHEST

NORM_EPS = 1e-6
DN_HEAD_DIM = 128
DN_CONV = 4
DN_CHUNK = 64
GM_GROUP_DIM = 64
GM_CHUNK = 128
SW_HEAD_DIM = 64
SW_PATTERNS = ((128, 1), (512, 4), (2048, 16))
SW_BLOCK = 128
ROPE_THETA = 500000.0
ROPE_DIM = SW_HEAD_DIM // 4
LANES = 128
AB_PAD = LANES

VMEM_LIMIT_BYTES = 56 * 1024 * 1024


def _params(n_parallel, n_arbitrary=0):
    return pltpu.CompilerParams(
        dimension_semantics=("parallel",) * n_parallel + ("arbitrary",) * n_arbitrary,
        vmem_limit_bytes=VMEM_LIMIT_BYTES,
    )


def _resident(shape):
    zeros = (0,) * len(shape)
    return pl.BlockSpec(shape, lambda *_: zeros, pipeline_mode=pl.Buffered(1))


def _sigmoid(x):
    return 1.0 / (1.0 + jnp.exp(-x))


def _silu(x):
    return x * _sigmoid(x)


def _dot(a, b):
    return jnp.dot(a, b, preferred_element_type=F32)


def _dot_nt(a, b):
    return lax.dot_general(a, b, (((1,), (1,)), ((), ())), preferred_element_type=F32)


def _dot_tn(a, b):
    return lax.dot_general(a, b, (((0,), (0,)), ((), ())), preferred_element_type=F32)


def _dot_hi(a, b):
    return jnp.dot(a, b, preferred_element_type=F32, precision=HIGHEST)


def _mod_kernel(c_ref, w_ref, b_ref, o_ref):
    c = c_ref[...]
    o_ref[...] = _dot_hi(_silu(c), w_ref[...]) + b_ref[...]


def _modulation(c, w_mod, b_mod):
    nl, d, e = w_mod.shape
    bsz = c.shape[0]
    tn = d
    return pl.pallas_call(
        _mod_kernel,
        grid=(nl, e // tn),
        in_specs=[
            pl.BlockSpec((bsz, d), lambda l, j: (0, 0)),
            pl.BlockSpec((None, d, tn), lambda l, j: (l, 0, j)),
            pl.BlockSpec((None, 1, tn), lambda l, j: (l, 0, j)),
        ],
        out_specs=pl.BlockSpec((None, bsz, tn), lambda l, j: (l, 0, j)),
        out_shape=jax.ShapeDtypeStruct((nl, bsz, e), F32),
        compiler_params=_params(2),
        name="modulation",
    )(c, w_mod, b_mod.reshape(nl, 1, e))


def _group_mean_sq(x, ones_bd, group):
    sq = x * x
    hi = sq.astype(BF16)
    lo = (sq - hi.astype(F32)).astype(BF16)
    return (_dot(hi, ones_bd) + _dot(lo, ones_bd)) * (1.0 / group)


def _rotary(x, cos_f, sin_a, sin_b):
    n = x.shape[-1]
    half = ROPE_DIM // 2
    return x * cos_f + pltpu.roll(x, n - half, 1) * sin_a + pltpu.roll(x, half, 1) * sin_b


def _in_kernel(x_ref, shift_ref, scale_ref, nw_ref, w_ref, cos_ref, sina_ref, sinb_ref,
               qnw_ref, knw_ref, bd_ref,
               dn_ref, z_ref, gm_ref, sw1_ref, sw2_ref, sw3_ref, ab_ref, stage_ref, *, widths):
    tm = x_ref.shape[0]
    x = x_ref[...]
    ms = jnp.mean(x * x, axis=-1, keepdims=True)
    h = x * lax.rsqrt(ms + NORM_EPS) * nw_ref[...] * (1.0 + scale_ref[...]) + shift_ref[...]
    hb = h.astype(BF16)
    offs = [0]
    for wd in widths:
        offs.append(offs[-1] + wd)
    plain = ((dn_ref, 0), (z_ref, 1), (gm_ref, 2), (ab_ref, 6))
    for ref, idx in plain:
        ref[...] = _dot(hb, w_ref[:, offs[idx]:offs[idx + 1]])
    sww = widths[3] // 3
    cos_f, sin_a, sin_b = cos_ref[...], sina_ref[...], sinb_ref[...]
    bd = bd_ref[...]
    for ref, idx in ((sw1_ref, 3), (sw2_ref, 4), (sw3_ref, 5)):
        raw = _dot(hb, w_ref[:, offs[idx]:offs[idx + 1]])
        q, k, v = raw[:, :sww], raw[:, sww:2 * sww], raw[:, 2 * sww:]
        qn = q * lax.rsqrt(_group_mean_sq(q, bd, SW_HEAD_DIM) + NORM_EPS) * qnw_ref[...]
        kn = k * lax.rsqrt(_group_mean_sq(k, bd, SW_HEAD_DIM) + NORM_EPS) * knw_ref[...]
        qn = _rotary(qn, cos_f, sin_a, sin_b) * (SW_HEAD_DIM ** -0.5)
        kn = _rotary(kn, cos_f, sin_a, sin_b)
        dilation = ref.shape[0]
        if dilation == 1:
            ref[0, :, :sww] = qn.astype(BF16)
            ref[0, :, sww:2 * sww] = kn.astype(BF16)
            ref[0, :, 2 * sww:] = v.astype(BF16)
        else:
            parts = (qn, kn, v)
            per = sww // LANES
            for j in range(3 * per):
                stage_ref[j] = parts[j // per][:, (j % per) * LANES:(j % per + 1) * LANES]
            for r in range(dilation):
                for j in range(3 * per):
                    ref[r, :, j * LANES:(j + 1) * LANES] = stage_ref[
                        j, pl.ds(r, tm // dilation, stride=dilation), :].astype(BF16)


def _in_projection(x, mod4, nw, w_cat, tables, qnw, knw, ones_bd, widths, tm):
    bsz, t, d = x.shape
    cos_f, sin_a, sin_b = tables
    sww = widths[3] // 3
    row = lambda wd: pl.BlockSpec((None, tm, wd), lambda b, i: (b, i, 0))
    modspec = lambda k: pl.BlockSpec((None, None, 1, d), lambda b, i: (b, k, 0, 0))
    tabspec = pl.BlockSpec((tm, sww), lambda b, i: (i, 0))
    sw_specs = [pl.BlockSpec((None, dil, tm // dil, 3 * sww), lambda b, i: (b, 0, i, 0))
                for _, dil in SW_PATTERNS]
    sw_shapes = [jax.ShapeDtypeStruct((bsz, dil, t // dil, 3 * sww), BF16) for _, dil in SW_PATTERNS]
    f32_out = lambda wd: jax.ShapeDtypeStruct((bsz, t, wd), F32)
    return pl.pallas_call(
        functools.partial(_in_kernel, widths=widths),
        grid=(bsz, t // tm),
        in_specs=[
            row(d), modspec(0), modspec(1), _resident(nw.shape), _resident(w_cat.shape),
            tabspec, tabspec, tabspec, _resident(qnw.shape), _resident(knw.shape),
            _resident(ones_bd.shape),
        ],
        out_specs=[row(widths[0]), row(widths[1]), row(widths[2])] + sw_specs + [row(widths[6])],
        out_shape=[f32_out(widths[0]), f32_out(widths[1]), f32_out(widths[2])] + sw_shapes
        + [f32_out(widths[6])],
        scratch_shapes=[pltpu.VMEM((3 * sww // LANES, tm, LANES), F32)],
        compiler_params=_params(2),
        name="in_projection",
    )(x, mod4, mod4, nw, w_cat, cos_f, sin_a, sin_b, qnw, knw, ones_bd)


def _expand(x, e):
    x1 = x.astype(BF16)
    r = x - x1.astype(F32)
    x2 = r.astype(BF16)
    x3 = (r - x2.astype(F32)).astype(BF16)
    return _dot(x1, e) + _dot(x2, e) + _dot(x3, e)


def _block_diag(x, mask):
    reps = mask.shape[0] // x.shape[0]
    return jnp.where(mask, jnp.tile(x, (reps, 1)), jnp.zeros((), x.dtype))


def _split(x):
    hi = x.astype(BF16)
    return hi, (x - hi.astype(F32)).astype(BF16)


def _dot_split(a, b_cat, mask):
    a_hi, a_lo = _split(a)
    b_hi, b_lo = _split(b_cat)
    n = a.shape[0]
    r = _dot(jnp.concatenate([a_hi, a_lo], axis=0), _block_diag(b_hi, mask))
    return r[:n] + r[n:] + _dot(a_hi, _block_diag(b_lo, mask))


def _dn_kernel(qkv_ref, z_ref, ab_ref, cw_ref, alog_ref, dtb_ref, onw_ref, eg64_ref, eg128_ref,
               eb128_ref, o_ref, xp_ref, q_s, qd_s, k_s, kb_s, kd_s, vb_s, wr_s, eg_s, gcat_s,
               s_ref, *, tt, chunk, heads):
    hd = DN_HEAD_DIM
    width = heads * hd
    cat = heads * chunk
    nchunks = tt // chunk
    pad = 8
    assert heads % 2 == 0 and 2 * hd == cat

    @pl.when(pl.program_id(1) == 0)
    def _():
        xp_ref[0:pad, :] = jnp.zeros((pad, 3 * width), F32)
        s_ref[...] = jnp.zeros_like(s_ref)

    xp_ref[pad:pad + tt, :] = qkv_ref[...]

    ab = ab_ref[...]
    sp = ab + dtb_ref[...]
    softplus = jnp.maximum(sp, 0.0) + jnp.log1p(jnp.exp(-jnp.abs(sp)))
    g = -jnp.exp(alog_ref[...]) * softplus
    pos = lax.broadcasted_iota(jnp.int32, g.shape, 0) % chunk
    step = 1
    while step < chunk:
        g = g + jnp.where(pos >= step, pltpu.roll(g, step, 0), 0.0)
        step *= 2
    gcat_s[...] = _expand(g, eg64_ref[...])
    gexp = _expand(g, eg128_ref[...])
    bexp = _expand(_sigmoid(ab), eb128_ref[...])
    eg = jnp.exp(gexp)
    eg_s[...] = eg
    g_last = jnp.concatenate(
        [jnp.broadcast_to(gexp[(c + 1) * chunk - 1:(c + 1) * chunk, :], (chunk, width))
         for c in range(nchunks)], axis=0)
    kd_scale = jnp.exp(g_last - gexp)

    def conv(c0):
        cols = slice(c0, c0 + hd)
        y = cw_ref[DN_CONV - 1:DN_CONV, cols] * xp_ref[pad:pad + tt, cols]
        for tap in range(DN_CONV - 1):
            lag = DN_CONV - 1 - tap
            y = y + cw_ref[tap:tap + 1, cols] * xp_ref[pad - lag:pad - lag + tt, cols]
        return _silu(y)

    for h in range(heads):
        hc = slice(h * hd, (h + 1) * hd)
        q = conv(h * hd)
        q = q * lax.rsqrt(jnp.sum(q * q, axis=-1, keepdims=True) + NORM_EPS) * (hd ** -0.5)
        k = conv(width + h * hd)
        k = k * lax.rsqrt(jnp.sum(k * k, axis=-1, keepdims=True) + NORM_EPS)
        v = conv(2 * width + h * hd)
        kb = k * bexp[:, hc]
        q_s[:, hc] = q.astype(BF16)
        qd_s[:, hc] = (q * eg[:, hc]).astype(BF16)
        k_s[:, hc] = k.astype(BF16)
        kb_s[:, hc] = kb.astype(BF16)
        kd_s[:, hc] = (k * kd_scale[:, hc]).astype(BF16)
        vb_s[:, hc] = (v * bexp[:, hc]).astype(BF16)
        wr_s[:, hc] = (kb * eg[:, hc]).astype(BF16)
    xp_ref[0:pad, :] = xp_ref[tt:tt + pad, :]

    ri = lax.broadcasted_iota(jnp.int32, (chunk, cat), 0)
    li = lax.broadcasted_iota(jnp.int32, (chunk, cat), 1) % chunk
    causal = ri >= li
    strict = ri > li
    diag = ri == li
    eye_cat = jnp.where(diag, 1.0, 0.0).astype(F32)
    rb = lax.broadcasted_iota(jnp.int32, (cat, width), 0) // chunk
    mask_w = rb == lax.broadcasted_iota(jnp.int32, (cat, width), 1) // hd
    rc = lax.broadcasted_iota(jnp.int32, (cat, cat), 0)
    cc = lax.broadcasted_iota(jnp.int32, (cat, cat), 1)
    mask_c = (rc // chunk) == (cc // chunk)
    mask_pair = (rc // hd) == (cc // hd)
    n_double = int(math.log2(chunk)) - 1
    rows = [slice(c * chunk, (c + 1) * chunk) for c in range(nchunks)]

    inv, pw, a_qk = [], [], []
    for c in range(nchunks):
        k_bd = _block_diag(k_s[rows[c], :], mask_w)
        both = _dot_nt(jnp.concatenate([kb_s[rows[c], :], q_s[rows[c], :]], axis=0), k_bd)
        gc = gcat_s[rows[c], :]
        g_row = jnp.sum(jnp.where(diag, gc, 0.0), axis=0, keepdims=True)
        decay = jnp.where(causal, jnp.exp(gc - g_row), 0.0)
        a_low = jnp.where(strict, both[:chunk] * decay, 0.0)
        a_qk.append((both[chunk:] * decay).astype(BF16))
        inv.append(eye_cat - a_low)
        pw.append(_dot_split(a_low, a_low, mask_c))
    for it in range(n_double):
        for c in range(nchunks):
            if it + 1 < n_double:
                r = _dot_split(jnp.concatenate([inv[c], pw[c]], axis=0), pw[c], mask_c)
                inv[c] = inv[c] + r[:chunk]
                pw[c] = r[chunk:]
            else:
                inv[c] = inv[c] + _dot_split(inv[c], pw[c], mask_c)
    u, w = [], []
    for c in range(nchunks):
        inv_hi, inv_lo = _split(inv[c])
        inv_b = jnp.concatenate([inv_hi, inv_lo], axis=0)
        ru = _dot(inv_b, _block_diag(vb_s[rows[c], :], mask_w))
        rw = _dot(inv_b, _block_diag(wr_s[rows[c], :], mask_w))
        u.append(ru[:chunk] + ru[chunk:])
        w.append((rw[:chunk] + rw[chunk:]).astype(BF16))

    npair = heads // 2
    pcols = [slice(p * cat, (p + 1) * cat) for p in range(npair)]
    state = [s_ref[p] for p in range(npair)]
    for c in range(nchunks):
        res = [_dot(jnp.concatenate([w[c][:, pcols[p]], qd_s[rows[c], pcols[p]]], axis=0),
                    state[p].astype(BF16)) for p in range(npair)]
        v_new = u[c] - jnp.concatenate([r[:chunk] for r in res], axis=1)
        v_new_b = v_new.astype(BF16)
        o = (jnp.concatenate([r[chunk:] for r in res], axis=1)
             + _dot(a_qk[c], _block_diag(v_new_b, mask_w)))
        o_ref[rows[c], :] = o
        last = (c + 1) * chunk - 1
        for p in range(npair):
            upd = _dot_tn(kd_s[rows[c], pcols[p]], v_new_b[:, pcols[p]])
            state[p] = state[p] * eg_s[last:last + 1, pcols[p]] + jnp.where(mask_pair, upd, 0.0)
    for p in range(npair):
        s_ref[p] = state[p]

    for h in range(heads):
        hc = slice(h * hd, (h + 1) * hd)
        o = o_ref[:, hc]
        on = o * lax.rsqrt(jnp.mean(o * o, axis=-1, keepdims=True) + NORM_EPS) * onw_ref[...]
        o_ref[:, hc] = on * _silu(z_ref[:, hc])


def _deltanet(qkv, z, ab, conv_w, alog_pad, dtb_pad, onw, tt):
    bsz, t, w3 = qkv.shape
    width = w3 // 3
    heads = width // DN_HEAD_DIM
    chunk = DN_CHUNK
    cat = heads * chunk
    hh = jnp.arange(AB_PAD)[:, None]
    e_g64 = (hh == jnp.arange(cat)[None, :] // chunk).astype(BF16)
    e_g128 = (hh == jnp.arange(width)[None, :] // DN_HEAD_DIM).astype(BF16)
    e_b128 = (hh == heads + jnp.arange(width)[None, :] // DN_HEAD_DIM).astype(BF16)
    row = lambda wd: pl.BlockSpec((None, tt, wd), lambda b, i: (b, i, 0))
    return pl.pallas_call(
        functools.partial(_dn_kernel, tt=tt, chunk=chunk, heads=heads),
        grid=(bsz, t // tt),
        in_specs=[row(w3), row(width), row(AB_PAD), _resident(conv_w.shape),
                  _resident(alog_pad.shape), _resident(dtb_pad.shape), _resident(onw.shape),
                  _resident(e_g64.shape), _resident(e_g128.shape), _resident(e_b128.shape)],
        out_specs=row(width),
        out_shape=jax.ShapeDtypeStruct((bsz, t, width), F32),
        scratch_shapes=[pltpu.VMEM((tt + 8, w3), F32)]
        + [pltpu.VMEM((tt, width), BF16)] * 7
        + [pltpu.VMEM((tt, width), F32), pltpu.VMEM((tt, cat), F32),
           pltpu.VMEM((heads // 2, cat, cat), F32)],
        compiler_params=_params(1, 1),
        name="deltanet",
    )(qkv, z, ab, conv_w, alog_pad, dtb_pad, onw, e_g64, e_g128, e_b128)


def _gm_kernel(uv_ref, lng_ref, lnb_ref, ws_ref, bias_ref, o_ref, *, tt, groups):
    width = groups * GM_GROUP_DIM
    x = uv_ref[...]
    zz = 0.5 * x * (1.0 + lax.erf(x * (2.0 ** -0.5)))
    u, v = zz[:, :width], zz[:, width:]
    mu = jnp.mean(v, axis=-1, keepdims=True)
    vc = v - mu
    var = jnp.mean(vc * vc, axis=-1, keepdims=True)
    vn = (vc * lax.rsqrt(var + NORM_EPS) * lng_ref[...] + lnb_ref[...]).astype(BF16)
    ii = lax.broadcasted_iota(jnp.int32, (GM_CHUNK, GM_CHUNK), 0)
    jj = lax.broadcasted_iota(jnp.int32, (GM_CHUNK, GM_CHUNK), 1)
    ws = [jnp.where(ii >= jj, ws_ref[g], 0.0).astype(BF16) for g in range(groups)]
    bias = bias_ref[...]
    for c in range(tt // GM_CHUNK):
        rows = slice(c * GM_CHUNK, (c + 1) * GM_CHUNK)
        parts = [_dot(ws[g], vn[rows, g * GM_GROUP_DIM:(g + 1) * GM_GROUP_DIM])
                 for g in range(groups)]
        o_ref[rows, :] = u[rows, :] * (jnp.concatenate(parts, axis=1) + bias)


def _spatial_gating(uv, ln_g, ln_b, w_s, bias_full, tt):
    bsz, t, w2 = uv.shape
    width = w2 // 2
    groups = width // GM_GROUP_DIM
    return pl.pallas_call(
        functools.partial(_gm_kernel, tt=tt, groups=groups),
        grid=(bsz, t // tt),
        in_specs=[pl.BlockSpec((None, tt, w2), lambda b, i: (b, i, 0)),
                  _resident(ln_g.shape), _resident(ln_b.shape), _resident(w_s.shape),
                  _resident(bias_full.shape)],
        out_specs=pl.BlockSpec((None, tt, width), lambda b, i: (b, i, 0)),
        out_shape=jax.ShapeDtypeStruct((bsz, t, width), F32),
        compiler_params=_params(2),
        name="spatial_gating",
    )(uv, ln_g, ln_b, w_s, bias_full)


def _sw_kernel(q_ref, k_ref, v_ref, kp_ref, vp_ref, o_ref, lse_ref, *, dilation, heads, span):
    blk = SW_BLOCK
    hd = SW_HEAD_DIM
    nqb = q_ref.shape[1] // blk
    a = lax.broadcasted_iota(jnp.int32, (blk, 2 * blk), 0)
    c = lax.broadcasted_iota(jnp.int32, (blk, 2 * blk), 1)
    dist = blk + a - c
    in_window = (dist >= 0) & (dist <= span)
    for r in range(dilation):
        for qb in range(nqb):
            rows = slice(qb * blk, (qb + 1) * blk)
            q = q_ref[r, rows, :]
            if qb == 0:
                kk = jnp.concatenate([kp_ref[r], k_ref[r, rows, :]], axis=0)
                vv = jnp.concatenate([vp_ref[r], v_ref[r, rows, :]], axis=0)
            else:
                kk = k_ref[r, (qb - 1) * blk:(qb + 1) * blk, :]
                vv = v_ref[r, (qb - 1) * blk:(qb + 1) * blk, :]
            has_prev = (pl.program_id(1) * nqb + qb) > 0
            valid = in_window & ((c >= blk) | has_prev)
            outs, lses = [], []
            for h in range(heads):
                hc = slice(h * hd, (h + 1) * hd)
                s = jnp.where(valid, _dot_nt(q[:, hc], kk[:, hc]), -jnp.inf)
                m = jnp.max(s, axis=-1, keepdims=True)
                p = jnp.exp(s - m)
                l = jnp.sum(p, axis=-1, keepdims=True)
                outs.append(_dot(p.astype(BF16), vv[:, hc]) / l)
                lses.append(jnp.broadcast_to(m + jnp.log(l), (blk, hd)))
            if dilation == 1:
                tok = rows
            else:
                tok = pl.ds(r + dilation * qb * blk, blk, stride=dilation)
            o_all = jnp.concatenate(outs, axis=1)
            lse_all = jnp.concatenate(lses, axis=1)
            for j in range(o_ref.shape[0]):
                o_ref[j, tok, :] = o_all[:, j * LANES:(j + 1) * LANES]
                lse_ref[j, tok, :] = lse_all[:, j * LANES:(j + 1) * LANES]


def _window_attention(qkv, window, dilation):
    bsz, _, length, w3 = qkv.shape
    t = length * dilation
    width = w3 // 3
    heads = width // SW_HEAD_DIM
    span = window // dilation
    assert span <= SW_BLOCK and length % SW_BLOCK == 0
    nb = max(1, 4 // dilation)
    rl = nb * SW_BLOCK
    cur = lambda j: pl.BlockSpec((None, dilation, rl, width), lambda b, i: (b, 0, i, j))
    prev = lambda j: pl.BlockSpec((None, dilation, SW_BLOCK, width),
                                  lambda b, i: (b, 0, jnp.maximum(i * nb - 1, 0), j))
    out = pl.BlockSpec((None, width // LANES, rl * dilation, LANES), lambda b, i: (b, 0, i, 0))
    return pl.pallas_call(
        functools.partial(_sw_kernel, dilation=dilation, heads=heads, span=span),
        grid=(bsz, length // rl),
        in_specs=[cur(0), cur(1), cur(2), prev(1), prev(2)],
        out_specs=[out, out],
        out_shape=[jax.ShapeDtypeStruct((bsz, width // LANES, t, LANES), F32)] * 2,
        compiler_params=_params(2),
        name=f"window_attention_d{dilation}",
    )(qkv, qkv, qkv, qkv, qkv)


def _out_kernel(x_ref, ya_ref, yb_ref, o1_ref, l1_ref, o2_ref, l2_ref, o3_ref, l3_ref,
                gate1_ref, shift2_ref, scale2_ref, gate2_ref, fnw_ref,
                wout_ref, wfi_ref, wfo_ref, out_ref, *, ffn_chunks):
    ycs = []
    for j in range(o1_ref.shape[0]):
        l1, l2, l3 = l1_ref[j], l2_ref[j], l3_ref[j]
        m = jnp.maximum(jnp.maximum(l1, l2), l3)
        e1, e2, e3 = jnp.exp(l1 - m), jnp.exp(l2 - m), jnp.exp(l3 - m)
        yc = (e1 * o1_ref[j] + e2 * o2_ref[j] + e3 * o3_ref[j]) / (e1 + e2 + e3)
        ycs.append(yc.astype(BF16))
    y = jnp.concatenate([ya_ref[...].astype(BF16), yb_ref[...].astype(BF16)] + ycs, axis=1)
    x1 = x_ref[...] + gate1_ref[...] * _dot(y, wout_ref[...])
    ms = jnp.mean(x1 * x1, axis=-1, keepdims=True)
    h = x1 * lax.rsqrt(ms + NORM_EPS) * fnw_ref[...] * (1.0 + scale2_ref[...]) + shift2_ref[...]
    hb = h.astype(BF16)
    hidden = wfo_ref.shape[0]
    fc = hidden // ffn_chunks
    acc = None
    for j in range(ffn_chunks):
        gate = _dot(hb, wfi_ref[:, j * fc:(j + 1) * fc])
        up = _dot(hb, wfi_ref[:, hidden + j * fc:hidden + (j + 1) * fc])
        part = _dot((_silu(gate) * up).astype(BF16), wfo_ref[j * fc:(j + 1) * fc, :])
        acc = part if acc is None else acc + part
    out_ref[...] = x1 + gate2_ref[...] * acc


def _out_ffn(x, ya, yb, attn, mod4, fnw, wout, wfi, wfo, tm):
    bsz, t, d = x.shape
    row = lambda wd: pl.BlockSpec((None, tm, wd), lambda b, i: (b, i, 0))
    modspec = lambda k: pl.BlockSpec((None, None, 1, d), lambda b, i: (b, k, 0, 0))
    attn_flat = [a for pair in attn for a in pair]
    return pl.pallas_call(
        functools.partial(_out_kernel, ffn_chunks=2),
        grid=(bsz, t // tm),
        in_specs=[row(d), row(ya.shape[-1]), row(yb.shape[-1])]
        + [pl.BlockSpec((None, a.shape[1], tm, LANES), lambda b, i: (b, 0, i, 0)) for a in attn_flat]
        + [modspec(2), modspec(3), modspec(4), modspec(5), _resident(fnw.shape),
           _resident(wout.shape), _resident(wfi.shape), _resident(wfo.shape)],
        out_specs=row(d),
        out_shape=jax.ShapeDtypeStruct((bsz, t, d), F32),
        compiler_params=_params(2),
        name="out_ffn",
    )(x, ya, yb, *attn_flat, mod4, mod4, mod4, mod4, fnw, wout, wfi, wfo)


def _rotary_tables(t, heads):
    half = ROPE_DIM // 2
    inv = ROPE_THETA ** (-jnp.arange(0, ROPE_DIM, 2, dtype=F32) / ROPE_DIM)
    ang = jnp.arange(t, dtype=F32)[:, None] * inv[None, :]
    cos, sin = jnp.cos(ang), jnp.sin(ang)
    rest = SW_HEAD_DIM - ROPE_DIM
    ones = jnp.ones((t, rest), F32)
    zeros = jnp.zeros((t, rest), F32)
    zh = jnp.zeros((t, half), F32)
    cos_f = jnp.concatenate([cos, cos, ones], axis=1)
    sin_a = jnp.concatenate([-sin, zh, zeros], axis=1)
    sin_b = jnp.concatenate([zh, sin, zeros], axis=1)
    tile = lambda a: jnp.tile(a, (1, heads))
    return tile(cos_f), tile(sin_a), tile(sin_b)


def kernel(x, c, w_mod, b_mod, mix_norm_w, ffn_norm_w, w_in, w_out, dn_conv_w, dn_a_log,
           dn_dt_bias, dn_out_norm_w, gm_ln_g, gm_ln_b, gm_w_s, gm_b_s, sw_q_norm_w,
           sw_k_norm_w, w_ffn_in, w_ffn_out):
    bsz, t, d = x.shape
    nl = w_mod.shape[0]
    dn_heads = dn_a_log.shape[1]
    dn_width = dn_heads * DN_HEAD_DIM
    gm_groups = gm_w_s.shape[1]
    gm_width = gm_groups * GM_GROUP_DIM
    npat = len(SW_PATTERNS)
    sw_width = (w_in.shape[2] - 4 * dn_width - 2 * dn_heads - 2 * gm_width) // (3 * npat)
    sw_heads = sw_width // SW_HEAD_DIM
    assert 2 * dn_heads <= AB_PAD

    o_z = 3 * dn_width
    o_a = o_z + dn_width
    o_gm = o_a + 2 * dn_heads
    o_sw = o_gm + 2 * gm_width
    widths = (3 * dn_width, dn_width, 2 * gm_width) + (3 * sw_width,) * npat + (AB_PAD,)

    mod = _modulation(c, w_mod, b_mod)
    tables = _rotary_tables(t, sw_heads)
    ii = jnp.arange(sw_width) // SW_HEAD_DIM
    ones_bd = (ii[:, None] == ii[None, :]).astype(BF16)
    tm = 512

    for layer in range(nl):
        wl = w_in[layer]
        ab_cols = jnp.zeros((d, AB_PAD), F32).at[:, :2 * dn_heads].set(wl[:, o_a:o_gm])
        w_cat = jnp.concatenate(
            [wl[:, :o_a], wl[:, o_gm:o_sw], wl[:, o_sw:], ab_cols], axis=1).astype(BF16)
        mod4 = mod[layer].reshape(bsz, 6, 1, d)
        dn_qkv, dn_z, gm_uv, sw1, sw2, sw3, dn_ab = _in_projection(
            x, mod4, mix_norm_w[layer].reshape(1, d), w_cat, tables,
            jnp.tile(sw_q_norm_w[layer], sw_heads).reshape(1, sw_width),
            jnp.tile(sw_k_norm_w[layer], sw_heads).reshape(1, sw_width),
            ones_bd, widths, tm)

        pad_heads = lambda v: jnp.zeros((1, AB_PAD), F32).at[0, :dn_heads].set(v)
        y_a = _deltanet(dn_qkv, dn_z, dn_ab, dn_conv_w[layer], pad_heads(dn_a_log[layer]),
                        pad_heads(dn_dt_bias[layer]), dn_out_norm_w[layer].reshape(1, DN_HEAD_DIM),
                        tt=512)
        bias_full = jnp.repeat(gm_b_s[layer].T, GM_GROUP_DIM, axis=1)
        y_b = _spatial_gating(gm_uv, gm_ln_g[layer].reshape(1, gm_width),
                              gm_ln_b[layer].reshape(1, gm_width), gm_w_s[layer], bias_full,
                              tt=512)
        attn = [_window_attention(sw, window, dilation)
                for sw, (window, dilation) in zip((sw1, sw2, sw3), SW_PATTERNS)]
        x = _out_ffn(x, y_a, y_b, attn, mod4, ffn_norm_w[layer].reshape(1, d),
                     w_out[layer].astype(BF16), w_ffn_in[layer].astype(BF16),
                     w_ffn_out[layer].astype(BF16), tm)
    return x
```

```python
import functools
import math

import jax
import jax.numpy as jnp
from jax import lax
from jax.experimental import pallas as pl
from jax.experimental.pallas import tpu as pltpu

F32 = jnp.float32
BF16 = jnp.bfloat16
HIGHEST = lax.Precision.HIGHEST

NORM_EPS = 1e-6
DN_HEAD_DIM = 128
DN_CONV = 4
DN_CHUNK = 64
DN_INV_BASE = 16
GM_GROUP_DIM = 64
GM_CHUNK = 128
SW_HEAD_DIM = 64
SW_PATTERNS = ((128, 1), (512, 4), (2048, 16))
SW_BLOCK = 128
ROPE_THETA = 500000.0
ROPE_DIM = SW_HEAD_DIM // 4
LANES = 128
AB_PAD = LANES

VMEM_LIMIT_BYTES = 56 * 1024 * 1024


def _params(n_parallel, n_arbitrary=0):
    return pltpu.CompilerParams(
        dimension_semantics=("parallel",) * n_parallel + ("arbitrary",) * n_arbitrary,
        vmem_limit_bytes=VMEM_LIMIT_BYTES,
    )


def _resident(shape):
    zeros = (0,) * len(shape)
    return pl.BlockSpec(shape, lambda *_: zeros, pipeline_mode=pl.Buffered(1))


def _sigmoid(x):
    return 1.0 / (1.0 + jnp.exp(-x))


def _silu(x):
    return x * _sigmoid(x)


def _dot(a, b):
    return jnp.dot(a, b, preferred_element_type=F32)


def _dot_nt(a, b):
    return lax.dot_general(a, b, (((1,), (1,)), ((), ())), preferred_element_type=F32)


def _dot_tn(a, b):
    return lax.dot_general(a, b, (((0,), (0,)), ((), ())), preferred_element_type=F32)


def _dot_hi(a, b):
    return jnp.dot(a, b, preferred_element_type=F32, precision=HIGHEST)


def _mod_kernel(c_ref, w_ref, b_ref, o_ref):
    c = c_ref[...]
    o_ref[...] = _dot_hi(_silu(c), w_ref[...]) + b_ref[...]


def _modulation(c, w_mod, b_mod):
    nl, d, e = w_mod.shape
    bsz = c.shape[0]
    tn = d
    return pl.pallas_call(
        _mod_kernel,
        grid=(nl, e // tn),
        in_specs=[
            pl.BlockSpec((bsz, d), lambda l, j: (0, 0)),
            pl.BlockSpec((None, d, tn), lambda l, j: (l, 0, j)),
            pl.BlockSpec((None, 1, tn), lambda l, j: (l, 0, j)),
        ],
        out_specs=pl.BlockSpec((None, bsz, tn), lambda l, j: (l, 0, j)),
        out_shape=jax.ShapeDtypeStruct((nl, bsz, e), F32),
        compiler_params=_params(2),
        name="modulation",
    )(c, w_mod, b_mod.reshape(nl, 1, e))


def _group_mean_sq(x, ones_bd, group):
    sq = x * x
    hi = sq.astype(BF16)
    lo = (sq - hi.astype(F32)).astype(BF16)
    return (_dot(hi, ones_bd) + _dot(lo, ones_bd)) * (1.0 / group)


def _rotary(x, cos_f, sin_a, sin_b):
    n = x.shape[-1]
    half = ROPE_DIM // 2
    return x * cos_f + pltpu.roll(x, n - half, 1) * sin_a + pltpu.roll(x, half, 1) * sin_b


def _in_kernel(x_ref, shift_ref, scale_ref, nw_ref, w_ref, cos_ref, sina_ref, sinb_ref,
               qnw_ref, knw_ref, bd_ref,
               dn_ref, z_ref, gm_ref, sw1_ref, sw2_ref, sw3_ref, ab_ref, stage_ref, *, widths):
    tm = x_ref.shape[0]
    x = x_ref[...]
    ms = jnp.mean(x * x, axis=-1, keepdims=True)
    h = x * lax.rsqrt(ms + NORM_EPS) * nw_ref[...] * (1.0 + scale_ref[...]) + shift_ref[...]
    hb = h.astype(BF16)
    offs = [0]
    for wd in widths:
        offs.append(offs[-1] + wd)
    plain = ((dn_ref, 0), (z_ref, 1), (gm_ref, 2), (ab_ref, 6))
    for ref, idx in plain:
        ref[...] = _dot(hb, w_ref[:, offs[idx]:offs[idx + 1]])
    sww = widths[3] // 3
    cos_f, sin_a, sin_b = cos_ref[...], sina_ref[...], sinb_ref[...]
    bd = bd_ref[...]
    for ref, idx in ((sw1_ref, 3), (sw2_ref, 4), (sw3_ref, 5)):
        raw = _dot(hb, w_ref[:, offs[idx]:offs[idx + 1]])
        q, k, v = raw[:, :sww], raw[:, sww:2 * sww], raw[:, 2 * sww:]
        qn = q * lax.rsqrt(_group_mean_sq(q, bd, SW_HEAD_DIM) + NORM_EPS) * qnw_ref[...]
        kn = k * lax.rsqrt(_group_mean_sq(k, bd, SW_HEAD_DIM) + NORM_EPS) * knw_ref[...]
        qn = _rotary(qn, cos_f, sin_a, sin_b) * (SW_HEAD_DIM ** -0.5)
        kn = _rotary(kn, cos_f, sin_a, sin_b)
        dilation = ref.shape[0]
        if dilation == 1:
            ref[0, :, :sww] = qn.astype(BF16)
            ref[0, :, sww:2 * sww] = kn.astype(BF16)
            ref[0, :, 2 * sww:] = v.astype(BF16)
        else:
            parts = (qn, kn, v)
            per = sww // LANES
            for j in range(3 * per):
                stage_ref[j] = parts[j // per][:, (j % per) * LANES:(j % per + 1) * LANES]
            for r in range(dilation):
                for j in range(3 * per):
                    ref[r, :, j * LANES:(j + 1) * LANES] = stage_ref[
                        j, pl.ds(r, tm // dilation, stride=dilation), :].astype(BF16)


def _in_projection(x, mod4, nw, w_cat, tables, qnw, knw, ones_bd, widths, tm):
    bsz, t, d = x.shape
    cos_f, sin_a, sin_b = tables
    sww = widths[3] // 3
    row = lambda wd: pl.BlockSpec((None, tm, wd), lambda b, i: (b, i, 0))
    modspec = lambda k: pl.BlockSpec((None, None, 1, d), lambda b, i: (b, k, 0, 0))
    tabspec = pl.BlockSpec((tm, sww), lambda b, i: (i, 0))
    sw_specs = [pl.BlockSpec((None, dil, tm // dil, 3 * sww), lambda b, i: (b, 0, i, 0))
                for _, dil in SW_PATTERNS]
    sw_shapes = [jax.ShapeDtypeStruct((bsz, dil, t // dil, 3 * sww), BF16) for _, dil in SW_PATTERNS]
    f32_out = lambda wd: jax.ShapeDtypeStruct((bsz, t, wd), F32)
    return pl.pallas_call(
        functools.partial(_in_kernel, widths=widths),
        grid=(bsz, t // tm),
        in_specs=[
            row(d), modspec(0), modspec(1), _resident(nw.shape), _resident(w_cat.shape),
            tabspec, tabspec, tabspec, _resident(qnw.shape), _resident(knw.shape),
            _resident(ones_bd.shape),
        ],
        out_specs=[row(widths[0]), row(widths[1]), row(widths[2])] + sw_specs + [row(widths[6])],
        out_shape=[f32_out(widths[0]), f32_out(widths[1]), f32_out(widths[2])] + sw_shapes
        + [f32_out(widths[6])],
        scratch_shapes=[pltpu.VMEM((3 * sww // LANES, tm, LANES), F32)],
        compiler_params=_params(2),
        name="in_projection",
    )(x, mod4, mod4, nw, w_cat, cos_f, sin_a, sin_b, qnw, knw, ones_bd)


def _expand(x, e):
    x1 = x.astype(BF16)
    r = x - x1.astype(F32)
    x2 = r.astype(BF16)
    x3 = (r - x2.astype(F32)).astype(BF16)
    return _dot(x1, e) + _dot(x2, e) + _dot(x3, e)


def _block_diag(x, mask):
    reps = mask.shape[0] // x.shape[0]
    return jnp.where(mask, jnp.tile(x, (reps, 1)), jnp.zeros((), x.dtype))


def _dn_kernel(qkv_ref, z_ref, ab_ref, cw_ref, alog_ref, dtb_ref, onw_ref, eg64_ref, eg128_ref,
               eb128_ref, o_ref, xp_ref, q_s, qd_s, k_s, kb_s, kd_s, vb_s, wr_s, eg_s, gcat_s,
               s_ref, *, tt, chunk, heads):
    hd = DN_HEAD_DIM
    width = heads * hd
    cat = heads * chunk
    nchunks = tt // chunk
    pad = 8

    @pl.when(pl.program_id(1) == 0)
    def _():
        xp_ref[0:pad, :] = jnp.zeros((pad, 3 * width), F32)
        s_ref[...] = jnp.zeros_like(s_ref)

    xp_ref[pad:pad + tt, :] = qkv_ref[...]

    ab = ab_ref[...]
    sp = ab + dtb_ref[...]
    softplus = jnp.maximum(sp, 0.0) + jnp.log1p(jnp.exp(-jnp.abs(sp)))
    g = -jnp.exp(alog_ref[...]) * softplus
    pos = lax.broadcasted_iota(jnp.int32, g.shape, 0) % chunk
    step = 1
    while step < chunk:
        g = g + jnp.where(pos >= step, pltpu.roll(g, step, 0), 0.0)
        step *= 2
    gcat_s[...] = _expand(g, eg64_ref[...])
    gexp = _expand(g, eg128_ref[...])
    bexp = _expand(_sigmoid(ab), eb128_ref[...])
    eg = jnp.exp(gexp)
    eg_s[...] = eg
    g_last = jnp.concatenate(
        [jnp.broadcast_to(gexp[(c + 1) * chunk - 1:(c + 1) * chunk, :], (chunk, width))
         for c in range(nchunks)], axis=0)
    kd_scale = jnp.exp(g_last - gexp)

    def conv(c0):
        cols = slice(c0, c0 + hd)
        y = cw_ref[DN_CONV - 1:DN_CONV, cols] * xp_ref[pad:pad + tt, cols]
        for tap in range(DN_CONV - 1):
            lag = DN_CONV - 1 - tap
            y = y + cw_ref[tap:tap + 1, cols] * xp_ref[pad - lag:pad - lag + tt, cols]
        return _silu(y)

    for h in range(heads):
        hc = slice(h * hd, (h + 1) * hd)
        q = conv(h * hd)
        q = q * lax.rsqrt(jnp.sum(q * q, axis=-1, keepdims=True) + NORM_EPS) * (hd ** -0.5)
        k = conv(width + h * hd)
        k = k * lax.rsqrt(jnp.sum(k * k, axis=-1, keepdims=True) + NORM_EPS)
        v = conv(2 * width + h * hd)
        kb = k * bexp[:, hc]
        q_s[:, hc] = q.astype(BF16)
        qd_s[:, hc] = (q * eg[:, hc]).astype(BF16)
        k_s[:, hc] = k.astype(BF16)
        kb_s[:, hc] = kb.astype(BF16)
        kd_s[:, hc] = (k * kd_scale[:, hc]).astype(BF16)
        vb_s[:, hc] = (v * bexp[:, hc]).astype(BF16)
        wr_s[:, hc] = (kb * eg[:, hc]).astype(BF16)
    xp_ref[0:pad, :] = xp_ref[tt:tt + pad, :]

    ri = lax.broadcasted_iota(jnp.int32, (chunk, cat), 0)
    li = lax.broadcasted_iota(jnp.int32, (chunk, cat), 1) % chunk
    causal = ri >= li
    strict = ri > li
    diag = ri == li
    eye_cat = jnp.where(diag, 1.0, 0.0).astype(F32)
    rb = lax.broadcasted_iota(jnp.int32, (cat, width), 0) // chunk
    mask_w = rb == lax.broadcasted_iota(jnp.int32, (cat, width), 1) // hd
    rc = lax.broadcasted_iota(jnp.int32, (cat, cat), 0)
    cc = lax.broadcasted_iota(jnp.int32, (cat, cat), 1)
    mask_c = (rc // chunk) == (cc // chunk)
    rows = [slice(c * chunk, (c + 1) * chunk) for c in range(nchunks)]

    a_all, a_qk = [], []
    for c in range(nchunks):
        k_bd = _block_diag(k_s[rows[c], :], mask_w)
        both = _dot_nt(jnp.concatenate([kb_s[rows[c], :], q_s[rows[c], :]], axis=0), k_bd)
        gc = gcat_s[rows[c], :]
        g_row = jnp.sum(jnp.where(diag, gc, 0.0), axis=0, keepdims=True)
        decay = jnp.where(causal, jnp.exp(gc - g_row), 0.0)
        a_low = jnp.where(strict, both[:chunk] * decay, 0.0)
        a_qk.append((both[chunk:] * decay).astype(BF16))
        a_all.append(a_low)
    same = lambda s: (ri // s) == (li // s)
    in_base = same(DN_INV_BASE)
    inv, pw = [], []
    for c in range(nchunks):
        d_f = jnp.where(in_base, a_all[c], 0.0)
        d_b = d_f.astype(BF16)
        inv.append(eye_cat - d_f)
        pw.append(_dot(d_b, _block_diag(d_b, mask_c)))
    n_base = int(math.log2(DN_INV_BASE)) - 1
    for it in range(n_base):
        for c in range(nchunks):
            pw_b = pw[c].astype(BF16)
            pw_bd = _block_diag(pw_b, mask_c)
            if it + 1 < n_base:
                r = _dot(jnp.concatenate([inv[c].astype(BF16), pw_b], axis=0), pw_bd)
                inv[c] = inv[c] + r[:chunk]
                pw[c] = r[chunk:]
            else:
                inv[c] = inv[c] + _dot(inv[c].astype(BF16), pw_bd)
    size = DN_INV_BASE
    while size < chunk:
        off_diag = same(2 * size) & jnp.logical_not(same(size))
        for c in range(nchunks):
            t_b = inv[c].astype(BF16)
            n_b = jnp.where(off_diag, a_all[c], 0.0).astype(BF16)
            t_n = _dot(t_b, _block_diag(n_b, mask_c))
            inv[c] = inv[c] - _dot(t_n.astype(BF16), _block_diag(t_b, mask_c))
        size *= 2
    hcols = [slice(h * hd, (h + 1) * hd) for h in range(heads)]
    hrows = [slice(h * chunk, (h + 1) * chunk) for h in range(heads)]
    stack = lambda ref, c: jnp.concatenate([ref[rows[c], hcols[h]] for h in range(heads)], axis=0)
    u, w = [], []
    for c in range(nchunks):
        rhs = jnp.concatenate([stack(vb_s, c), stack(wr_s, c)], axis=1)
        uw = _dot(_block_diag(inv[c].astype(BF16), mask_c), rhs)
        u.append(uw[:, :hd])
        w.append(uw[:, hd:].astype(BF16))

    state = [s_ref[h] for h in range(heads)]
    for c in range(nchunks):
        res = [_dot(jnp.concatenate([w[c][hrows[h], :], qd_s[rows[c], hcols[h]]], axis=0),
                    state[h].astype(BF16)) for h in range(heads)]
        v_new = u[c] - jnp.concatenate([r[:chunk] for r in res], axis=0)
        v_new_b = v_new.astype(BF16)
        o = (jnp.concatenate([r[chunk:] for r in res], axis=0)
             + _dot(_block_diag(a_qk[c], mask_c), v_new_b))
        last = (c + 1) * chunk - 1
        for h in range(heads):
            o_ref[rows[c], hcols[h]] = o[hrows[h], :]
            upd = _dot_tn(kd_s[rows[c], hcols[h]], v_new_b[hrows[h], :])
            state[h] = state[h] * eg_s[last:last + 1, hcols[h]] + upd
    for h in range(heads):
        s_ref[h] = state[h]

    for h in range(heads):
        hc = slice(h * hd, (h + 1) * hd)
        o = o_ref[:, hc]
        on = o * lax.rsqrt(jnp.mean(o * o, axis=-1, keepdims=True) + NORM_EPS) * onw_ref[...]
        o_ref[:, hc] = on * _silu(z_ref[:, hc])


def _deltanet(qkv, z, ab, conv_w, alog_pad, dtb_pad, onw, tt):
    bsz, t, w3 = qkv.shape
    width = w3 // 3
    heads = width // DN_HEAD_DIM
    chunk = DN_CHUNK
    cat = heads * chunk
    hh = jnp.arange(AB_PAD)[:, None]
    e_g64 = (hh == jnp.arange(cat)[None, :] // chunk).astype(BF16)
    e_g128 = (hh == jnp.arange(width)[None, :] // DN_HEAD_DIM).astype(BF16)
    e_b128 = (hh == heads + jnp.arange(width)[None, :] // DN_HEAD_DIM).astype(BF16)
    row = lambda wd: pl.BlockSpec((None, tt, wd), lambda b, i: (b, i, 0))
    return pl.pallas_call(
        functools.partial(_dn_kernel, tt=tt, chunk=chunk, heads=heads),
        grid=(bsz, t // tt),
        in_specs=[row(w3), row(width), row(AB_PAD), _resident(conv_w.shape),
                  _resident(alog_pad.shape), _resident(dtb_pad.shape), _resident(onw.shape),
                  _resident(e_g64.shape), _resident(e_g128.shape), _resident(e_b128.shape)],
        out_specs=row(width),
        out_shape=jax.ShapeDtypeStruct((bsz, t, width), F32),
        scratch_shapes=[pltpu.VMEM((tt + 8, w3), F32)]
        + [pltpu.VMEM((tt, width), BF16)] * 7
        + [pltpu.VMEM((tt, width), F32), pltpu.VMEM((tt, cat), F32),
           pltpu.VMEM((heads, DN_HEAD_DIM, DN_HEAD_DIM), F32)],
        compiler_params=_params(1, 1),
        name="deltanet",
    )(qkv, z, ab, conv_w, alog_pad, dtb_pad, onw, e_g64, e_g128, e_b128)


def _gm_kernel(uv_ref, lng_ref, lnb_ref, ws_ref, bias_ref, o_ref, *, tt, groups):
    width = groups * GM_GROUP_DIM
    x = uv_ref[...]
    zz = 0.5 * x * (1.0 + lax.erf(x * (2.0 ** -0.5)))
    u, v = zz[:, :width], zz[:, width:]
    mu = jnp.mean(v, axis=-1, keepdims=True)
    vc = v - mu
    var = jnp.mean(vc * vc, axis=-1, keepdims=True)
    vn = (vc * lax.rsqrt(var + NORM_EPS) * lng_ref[...] + lnb_ref[...]).astype(BF16)
    ii = lax.broadcasted_iota(jnp.int32, (GM_CHUNK, GM_CHUNK), 0)
    jj = lax.broadcasted_iota(jnp.int32, (GM_CHUNK, GM_CHUNK), 1)
    ws = [jnp.where(ii >= jj, ws_ref[g], 0.0).astype(BF16) for g in range(groups)]
    bias = bias_ref[...]
    for c in range(tt // GM_CHUNK):
        rows = slice(c * GM_CHUNK, (c + 1) * GM_CHUNK)
        parts = [_dot(ws[g], vn[rows, g * GM_GROUP_DIM:(g + 1) * GM_GROUP_DIM])
                 for g in range(groups)]
        o_ref[rows, :] = u[rows, :] * (jnp.concatenate(parts, axis=1) + bias)


def _spatial_gating(uv, ln_g, ln_b, w_s, bias_full, tt):
    bsz, t, w2 = uv.shape
    width = w2 // 2
    groups = width // GM_GROUP_DIM
    return pl.pallas_call(
        functools.partial(_gm_kernel, tt=tt, groups=groups),
        grid=(bsz, t // tt),
        in_specs=[pl.BlockSpec((None, tt, w2), lambda b, i: (b, i, 0)),
                  _resident(ln_g.shape), _resident(ln_b.shape), _resident(w_s.shape),
                  _resident(bias_full.shape)],
        out_specs=pl.BlockSpec((None, tt, width), lambda b, i: (b, i, 0)),
        out_shape=jax.ShapeDtypeStruct((bsz, t, width), F32),
        compiler_params=_params(2),
        name="spatial_gating",
    )(uv, ln_g, ln_b, w_s, bias_full)


def _sw_kernel(q_ref, k_ref, v_ref, kp_ref, vp_ref, o_ref, lse_ref, *, dilation, heads, span):
    blk = SW_BLOCK
    hd = SW_HEAD_DIM
    nk = 2 * blk
    nqb = q_ref.shape[1] // blk
    assert 2 * hd == LANES and heads % 2 == 0
    a = lax.broadcasted_iota(jnp.int32, (blk, 2 * nk), 0)
    c = lax.broadcasted_iota(jnp.int32, (blk, 2 * nk), 1) % nk
    dist = blk + a - c
    in_window = (dist >= 0) & (dist <= span)
    own_block = c >= blk
    mask_kv = (lax.broadcasted_iota(jnp.int32, (2 * nk, LANES), 0) // nk
               == lax.broadcasted_iota(jnp.int32, (2 * nk, LANES), 1) // hd)
    ones_bd = mask_kv.astype(BF16)
    units = [(r, qb, j) for r in range(dilation) for qb in range(nqb) for j in range(heads // 2)]
    group = 4
    for g0 in range(0, len(units), group):
        grp = units[g0:g0 + group]
        ss, vos = [], []
        for r, qb, j in grp:
            rows = slice(qb * blk, (qb + 1) * blk)
            lanes = slice(j * LANES, (j + 1) * LANES)
            if qb == 0:
                kk = jnp.concatenate([kp_ref[r, :, lanes], k_ref[r, rows, lanes]], axis=0)
                vv = jnp.concatenate([vp_ref[r, :, lanes], v_ref[r, rows, lanes]], axis=0)
            else:
                kk = k_ref[r, (qb - 1) * blk:(qb + 1) * blk, lanes]
                vv = v_ref[r, (qb - 1) * blk:(qb + 1) * blk, lanes]
            s = _dot_nt(q_ref[r, rows, lanes], _block_diag(kk, mask_kv))
            has_prev = (pl.program_id(1) * nqb + qb) > 0
            ss.append(jnp.where(in_window & (own_block | has_prev), s, -jnp.inf))
            vos.append(jnp.concatenate([_block_diag(vv, mask_kv), ones_bd], axis=1))
        ms, ps = [], []
        for s in ss:
            m_a = jnp.max(s[:, :nk], axis=-1, keepdims=True)
            m_b = jnp.max(s[:, nk:], axis=-1, keepdims=True)
            m_wide = jnp.concatenate([jnp.broadcast_to(m_a, (blk, nk)),
                                      jnp.broadcast_to(m_b, (blk, nk))], axis=1)
            ps.append(jnp.exp(s - m_wide).astype(BF16))
            ms.append(jnp.concatenate([jnp.broadcast_to(m_a, (blk, hd)),
                                       jnp.broadcast_to(m_b, (blk, hd))], axis=1))
        for (r, qb, j), p, vo, m in zip(grp, ps, vos, ms):
            pv = _dot(p, vo)
            l = pv[:, LANES:]
            if dilation == 1:
                tok = slice(qb * blk, (qb + 1) * blk)
            else:
                tok = pl.ds(r + dilation * qb * blk, blk, stride=dilation)
            o_ref[j, tok, :] = pv[:, :LANES] / l
            lse_ref[j, tok, :] = m + jnp.log(l)


def _window_attention(qkv, window, dilation):
    bsz, _, length, w3 = qkv.shape
    t = length * dilation
    width = w3 // 3
    heads = width // SW_HEAD_DIM
    span = window // dilation
    assert span <= SW_BLOCK and length % SW_BLOCK == 0
    nb = max(1, 4 // dilation)
    rl = nb * SW_BLOCK
    cur = lambda j: pl.BlockSpec((None, dilation, rl, width), lambda b, i: (b, 0, i, j))
    prev = lambda j: pl.BlockSpec((None, dilation, SW_BLOCK, width),
                                  lambda b, i: (b, 0, jnp.maximum(i * nb - 1, 0), j))
    out = pl.BlockSpec((None, width // LANES, rl * dilation, LANES), lambda b, i: (b, 0, i, 0))
    return pl.pallas_call(
        functools.partial(_sw_kernel, dilation=dilation, heads=heads, span=span),
        grid=(bsz, length // rl),
        in_specs=[cur(0), cur(1), cur(2), prev(1), prev(2)],
        out_specs=[out, out],
        out_shape=[jax.ShapeDtypeStruct((bsz, width // LANES, t, LANES), F32)] * 2,
        compiler_params=_params(2),
        name=f"window_attention_d{dilation}",
    )(qkv, qkv, qkv, qkv, qkv)


def _out_kernel(x_ref, ya_ref, yb_ref, o1_ref, l1_ref, o2_ref, l2_ref, o3_ref, l3_ref,
                gate1_ref, shift2_ref, scale2_ref, gate2_ref, fnw_ref,
                wout_ref, wfi_ref, wfo_ref, out_ref, *, ffn_chunks):
    ycs = []
    for j in range(o1_ref.shape[0]):
        l1, l2, l3 = l1_ref[j], l2_ref[j], l3_ref[j]
        m = jnp.maximum(jnp.maximum(l1, l2), l3)
        e1, e2, e3 = jnp.exp(l1 - m), jnp.exp(l2 - m), jnp.exp(l3 - m)
        yc = (e1 * o1_ref[j] + e2 * o2_ref[j] + e3 * o3_ref[j]) / (e1 + e2 + e3)
        ycs.append(yc.astype(BF16))
    y = jnp.concatenate([ya_ref[...].astype(BF16), yb_ref[...].astype(BF16)] + ycs, axis=1)
    x1 = x_ref[...] + gate1_ref[...] * _dot(y, wout_ref[...])
    ms = jnp.mean(x1 * x1, axis=-1, keepdims=True)
    h = x1 * lax.rsqrt(ms + NORM_EPS) * fnw_ref[...] * (1.0 + scale2_ref[...]) + shift2_ref[...]
    hb = h.astype(BF16)
    hidden = wfo_ref.shape[0]
    fc = hidden // ffn_chunks
    acc = None
    for j in range(ffn_chunks):
        gate = _dot(hb, wfi_ref[:, j * fc:(j + 1) * fc])
        up = _dot(hb, wfi_ref[:, hidden + j * fc:hidden + (j + 1) * fc])
        part = _dot((_silu(gate) * up).astype(BF16), wfo_ref[j * fc:(j + 1) * fc, :])
        acc = part if acc is None else acc + part
    out_ref[...] = x1 + gate2_ref[...] * acc


def _out_ffn(x, ya, yb, attn, mod4, fnw, wout, wfi, wfo, tm):
    bsz, t, d = x.shape
    row = lambda wd: pl.BlockSpec((None, tm, wd), lambda b, i: (b, i, 0))
    modspec = lambda k: pl.BlockSpec((None, None, 1, d), lambda b, i: (b, k, 0, 0))
    attn_flat = [a for pair in attn for a in pair]
    return pl.pallas_call(
        functools.partial(_out_kernel, ffn_chunks=2),
        grid=(bsz, t // tm),
        in_specs=[row(d), row(ya.shape[-1]), row(yb.shape[-1])]
        + [pl.BlockSpec((None, a.shape[1], tm, LANES), lambda b, i: (b, 0, i, 0)) for a in attn_flat]
        + [modspec(2), modspec(3), modspec(4), modspec(5), _resident(fnw.shape),
           _resident(wout.shape), _resident(wfi.shape), _resident(wfo.shape)],
        out_specs=row(d),
        out_shape=jax.ShapeDtypeStruct((bsz, t, d), F32),
        compiler_params=_params(2),
        name="out_ffn",
    )(x, ya, yb, *attn_flat, mod4, mod4, mod4, mod4, fnw, wout, wfi, wfo)


def _rotary_tables(t, heads):
    half = ROPE_DIM // 2
    inv = ROPE_THETA ** (-jnp.arange(0, ROPE_DIM, 2, dtype=F32) / ROPE_DIM)
    ang = jnp.arange(t, dtype=F32)[:, None] * inv[None, :]
    cos, sin = jnp.cos(ang), jnp.sin(ang)
    rest = SW_HEAD_DIM - ROPE_DIM
    ones = jnp.ones((t, rest), F32)
    zeros = jnp.zeros((t, rest), F32)
    zh = jnp.zeros((t, half), F32)
    cos_f = jnp.concatenate([cos, cos, ones], axis=1)
    sin_a = jnp.concatenate([-sin, zh, zeros], axis=1)
    sin_b = jnp.concatenate([zh, sin, zeros], axis=1)
    tile = lambda a: jnp.tile(a, (1, heads))
    return tile(cos_f), tile(sin_a), tile(sin_b)


def kernel(x, c, w_mod, b_mod, mix_norm_w, ffn_norm_w, w_in, w_out, dn_conv_w, dn_a_log,
           dn_dt_bias, dn_out_norm_w, gm_ln_g, gm_ln_b, gm_w_s, gm_b_s, sw_q_norm_w,
           sw_k_norm_w, w_ffn_in, w_ffn_out):
    bsz, t, d = x.shape
    nl = w_mod.shape[0]
    dn_heads = dn_a_log.shape[1]
    dn_width = dn_heads * DN_HEAD_DIM
    gm_groups = gm_w_s.shape[1]
    gm_width = gm_groups * GM_GROUP_DIM
    npat = len(SW_PATTERNS)
    sw_width = (w_in.shape[2] - 4 * dn_width - 2 * dn_heads - 2 * gm_width) // (3 * npat)
    sw_heads = sw_width // SW_HEAD_DIM
    assert 2 * dn_heads <= AB_PAD

    o_z = 3 * dn_width
    o_a = o_z + dn_width
    o_gm = o_a + 2 * dn_heads
    o_sw = o_gm + 2 * gm_width
    widths = (3 * dn_width, dn_width, 2 * gm_width) + (3 * sw_width,) * npat + (AB_PAD,)

    mod = _modulation(c, w_mod, b_mod)
    tables = _rotary_tables(t, sw_heads)
    ii = jnp.arange(sw_width) // SW_HEAD_DIM
    ones_bd = (ii[:, None] == ii[None, :]).astype(BF16)
    tm = 512

    for layer in range(nl):
        wl = w_in[layer]
        ab_cols = jnp.zeros((d, AB_PAD), F32).at[:, :2 * dn_heads].set(wl[:, o_a:o_gm])
        w_cat = jnp.concatenate(
            [wl[:, :o_a], wl[:, o_gm:o_sw], wl[:, o_sw:], ab_cols], axis=1).astype(BF16)
        mod4 = mod[layer].reshape(bsz, 6, 1, d)
        dn_qkv, dn_z, gm_uv, sw1, sw2, sw3, dn_ab = _in_projection(
            x, mod4, mix_norm_w[layer].reshape(1, d), w_cat, tables,
            jnp.tile(sw_q_norm_w[layer], sw_heads).reshape(1, sw_width),
            jnp.tile(sw_k_norm_w[layer], sw_heads).reshape(1, sw_width),
            ones_bd, widths, tm)

        pad_heads = lambda v: jnp.zeros((1, AB_PAD), F32).at[0, :dn_heads].set(v)
        y_a = _deltanet(dn_qkv, dn_z, dn_ab, dn_conv_w[layer], pad_heads(dn_a_log[layer]),
                        pad_heads(dn_dt_bias[layer]), dn_out_norm_w[layer].reshape(1, DN_HEAD_DIM),
                        tt=512)
        bias_full = jnp.repeat(gm_b_s[layer].T, GM_GROUP_DIM, axis=1)
        y_b = _spatial_gating(gm_uv, gm_ln_g[layer].reshape(1, gm_width),
                              gm_ln_b[layer].reshape(1, gm_width), gm_w_s[layer], bias_full,
                              tt=512)
        attn = [_window_attention(sw, window, dilation)
                for sw, (window, dilation) in zip((sw1, sw2, sw3), SW_PATTERNS)]
        x = _out_ffn(x, y_a, y_b, attn, mod4, ffn_norm_w[layer].reshape(1, d),
                     w_out[layer].astype(BF16), w_ffn_in[layer].astype(BF16),
                     w_ffn_out[layer].astype(BF16), tm)
    return x
```

```python
import functools
import math

import jax
import jax.numpy as jnp
from jax import lax
from jax.experimental import pallas as pl
from jax.experimental.pallas import tpu as pltpu

F32 = jnp.float32
BF16 = jnp.bfloat16
HIGHEST = lax.Precision.HIGHEST

NORM_EPS = 1e-6
DN_HEAD_DIM = 128
DN_CONV = 4
DN_CHUNK = 64
DN_INV_BASE = 16
GM_GROUP_DIM = 64
GM_CHUNK = 128
SW_HEAD_DIM = 64
SW_PATTERNS = ((128, 1), (512, 4), (2048, 16))
SW_BLOCK = 128
ROPE_THETA = 500000.0
ROPE_DIM = SW_HEAD_DIM // 4
LANES = 128
AB_PAD = LANES

VMEM_LIMIT_BYTES = 56 * 1024 * 1024


def _params(n_parallel, n_arbitrary=0):
    return pltpu.CompilerParams(
        dimension_semantics=("parallel",) * n_parallel + ("arbitrary",) * n_arbitrary,
        vmem_limit_bytes=VMEM_LIMIT_BYTES,
    )


def _resident(shape):
    zeros = (0,) * len(shape)
    return pl.BlockSpec(shape, lambda *_: zeros, pipeline_mode=pl.Buffered(1))


def _sigmoid(x):
    return 1.0 / (1.0 + jnp.exp(-x))


def _silu(x):
    return x * _sigmoid(x)


def _dot(a, b):
    return jnp.dot(a, b, preferred_element_type=F32)


def _dot_nt(a, b):
    return lax.dot_general(a, b, (((1,), (1,)), ((), ())), preferred_element_type=F32)


def _dot_tn(a, b):
    return lax.dot_general(a, b, (((0,), (0,)), ((), ())), preferred_element_type=F32)


def _dot_hi(a, b):
    return jnp.dot(a, b, preferred_element_type=F32, precision=HIGHEST)


def _mod_kernel(c_ref, w_ref, b_ref, o_ref):
    c = c_ref[...]
    o_ref[...] = _dot_hi(_silu(c), w_ref[...]) + b_ref[...]


def _modulation(c, w_mod, b_mod):
    nl, d, e = w_mod.shape
    bsz = c.shape[0]
    tn = d
    return pl.pallas_call(
        _mod_kernel,
        grid=(nl, e // tn),
        in_specs=[
            pl.BlockSpec((bsz, d), lambda l, j: (0, 0)),
            pl.BlockSpec((None, d, tn), lambda l, j: (l, 0, j)),
            pl.BlockSpec((None, 1, tn), lambda l, j: (l, 0, j)),
        ],
        out_specs=pl.BlockSpec((None, bsz, tn), lambda l, j: (l, 0, j)),
        out_shape=jax.ShapeDtypeStruct((nl, bsz, e), F32),
        compiler_params=_params(2),
        name="modulation",
    )(c, w_mod, b_mod.reshape(nl, 1, e))


def _group_mean_sq(x, ones_bd, group):
    sq = x * x
    hi = sq.astype(BF16)
    lo = (sq - hi.astype(F32)).astype(BF16)
    return (_dot(hi, ones_bd) + _dot(lo, ones_bd)) * (1.0 / group)


def _rotary(x, cos_f, sin_a, sin_b):
    n = x.shape[-1]
    half = ROPE_DIM // 2
    return x * cos_f + pltpu.roll(x, n - half, 1) * sin_a + pltpu.roll(x, half, 1) * sin_b


def _in_kernel(x_ref, shift_ref, scale_ref, nw_ref, w_ref, cos_ref, sina_ref, sinb_ref,
               qnw_ref, knw_ref, bd_ref, cw_ref,
               dn_ref, z_ref, gm_ref, sw1_ref, sw2_ref, sw3_ref, ab_ref, stage_ref, hist_ref,
               *, widths):
    tm = x_ref.shape[0]
    x = x_ref[...]
    ms = jnp.mean(x * x, axis=-1, keepdims=True)
    h = x * lax.rsqrt(ms + NORM_EPS) * nw_ref[...] * (1.0 + scale_ref[...]) + shift_ref[...]
    hb = h.astype(BF16)
    offs = [0]
    for wd in widths:
        offs.append(offs[-1] + wd)
    plain = ((z_ref, 1), (gm_ref, 2), (ab_ref, 6))
    for ref, idx in plain:
        ref[...] = _dot(hb, w_ref[:, offs[idx]:offs[idx + 1]])

    pad = hist_ref.shape[0] - tm

    @pl.when(pl.program_id(1) == 0)
    def _():
        hist_ref[0:pad, :] = jnp.zeros((pad, widths[0]), F32)

    @pl.when(pl.program_id(1) > 0)
    def _():
        hist_ref[0:pad, :] = hist_ref[tm:tm + pad, :]

    hist_ref[pad:pad + tm, :] = _dot(hb, w_ref[:, offs[0]:offs[1]])
    dn_width = widths[0] // 3
    for c0 in range(0, widths[0], DN_HEAD_DIM):
        cols = slice(c0, c0 + DN_HEAD_DIM)
        y = cw_ref[DN_CONV - 1:DN_CONV, cols] * hist_ref[pad:pad + tm, cols]
        for tap in range(DN_CONV - 1):
            lag = DN_CONV - 1 - tap
            y = y + cw_ref[tap:tap + 1, cols] * hist_ref[pad - lag:pad - lag + tm, cols]
        y = _silu(y)
        if c0 < 2 * dn_width:
            y = y * lax.rsqrt(jnp.sum(y * y, axis=-1, keepdims=True) + NORM_EPS)
        if c0 < dn_width:
            y = y * (DN_HEAD_DIM ** -0.5)
        dn_ref[:, cols] = y
    sww = widths[3] // 3
    cos_f, sin_a, sin_b = cos_ref[...], sina_ref[...], sinb_ref[...]
    bd = bd_ref[...]
    for ref, idx in ((sw1_ref, 3), (sw2_ref, 4), (sw3_ref, 5)):
        raw = _dot(hb, w_ref[:, offs[idx]:offs[idx + 1]])
        q, k, v = raw[:, :sww], raw[:, sww:2 * sww], raw[:, 2 * sww:]
        qn = q * lax.rsqrt(_group_mean_sq(q, bd, SW_HEAD_DIM) + NORM_EPS) * qnw_ref[...]
        kn = k * lax.rsqrt(_group_mean_sq(k, bd, SW_HEAD_DIM) + NORM_EPS) * knw_ref[...]
        qn = _rotary(qn, cos_f, sin_a, sin_b) * (SW_HEAD_DIM ** -0.5)
        kn = _rotary(kn, cos_f, sin_a, sin_b)
        dilation = ref.shape[0]
        if dilation == 1:
            ref[0, :, :sww] = qn.astype(BF16)
            ref[0, :, sww:2 * sww] = kn.astype(BF16)
            ref[0, :, 2 * sww:] = v.astype(BF16)
        else:
            parts = (qn, kn, v)
            per = sww // LANES
            for j in range(3 * per):
                stage_ref[j] = parts[j // per][:, (j % per) * LANES:(j % per + 1) * LANES]
            for r in range(dilation):
                for j in range(3 * per):
                    ref[r, :, j * LANES:(j + 1) * LANES] = stage_ref[
                        j, pl.ds(r, tm // dilation, stride=dilation), :].astype(BF16)


def _in_projection(x, mod4, nw, w_cat, tables, qnw, knw, ones_bd, conv_w, widths, tm):
    bsz, t, d = x.shape
    cos_f, sin_a, sin_b = tables
    sww = widths[3] // 3
    row = lambda wd: pl.BlockSpec((None, tm, wd), lambda b, i: (b, i, 0))
    modspec = lambda k: pl.BlockSpec((None, None, 1, d), lambda b, i: (b, k, 0, 0))
    tabspec = pl.BlockSpec((tm, sww), lambda b, i: (i, 0))
    sw_specs = [pl.BlockSpec((None, dil, tm // dil, 3 * sww), lambda b, i: (b, 0, i, 0))
                for _, dil in SW_PATTERNS]
    sw_shapes = [jax.ShapeDtypeStruct((bsz, dil, t // dil, 3 * sww), BF16) for _, dil in SW_PATTERNS]
    f32_out = lambda wd: jax.ShapeDtypeStruct((bsz, t, wd), F32)
    return pl.pallas_call(
        functools.partial(_in_kernel, widths=widths),
        grid=(bsz, t // tm),
        in_specs=[
            row(d), modspec(0), modspec(1), _resident(nw.shape), _resident(w_cat.shape),
            tabspec, tabspec, tabspec, _resident(qnw.shape), _resident(knw.shape),
            _resident(ones_bd.shape), _resident(conv_w.shape),
        ],
        out_specs=[row(widths[0]), row(widths[1]), row(widths[2])] + sw_specs + [row(widths[6])],
        out_shape=[f32_out(widths[0]), f32_out(widths[1]), f32_out(widths[2])] + sw_shapes
        + [f32_out(widths[6])],
        scratch_shapes=[pltpu.VMEM((3 * sww // LANES, tm, LANES), F32),
                        pltpu.VMEM((tm + 8, widths[0]), F32)],
        compiler_params=_params(1, 1),
        name="in_projection",
    )(x, mod4, mod4, nw, w_cat, cos_f, sin_a, sin_b, qnw, knw, ones_bd, conv_w)


def _expand(x, e):
    x1 = x.astype(BF16)
    r = x - x1.astype(F32)
    x2 = r.astype(BF16)
    x3 = (r - x2.astype(F32)).astype(BF16)
    return _dot(x1, e) + _dot(x2, e) + _dot(x3, e)


def _block_diag(x, blk):
    c, n = x.shape
    zero = jnp.zeros((c, LANES), x.dtype)
    low = lax.broadcasted_iota(jnp.int32, (c, LANES), 1) < blk
    out = []
    for h in range(n // blk):
        g = (h * blk) // LANES
        piece = x[:, g * LANES:(g + 1) * LANES]
        if blk < LANES:
            piece = jnp.where(low if (h * blk) % LANES == 0 else jnp.logical_not(low), piece, zero)
        out.append(jnp.concatenate([piece if j == g else zero for j in range(n // LANES)], axis=1))
    return jnp.concatenate(out, axis=0)


def _dn_kernel(qkv_ref, z_ref, ab_ref, alog_ref, dtb_ref, onw_ref, eg64_ref, eg128_ref,
               eb128_ref, o_ref, q_s, qd_s, k_s, kb_s, kd_s, vb_s, wr_s, eg_s, gcat_s, s_ref,
               *, tt, chunk, heads):
    hd = DN_HEAD_DIM
    nseq = qkv_ref.shape[0]
    width = heads * hd
    cat = heads * chunk
    nchunks = tt // chunk
    assert hd == LANES and 2 * chunk == LANES

    @pl.when(pl.program_id(1) == 0)
    def _():
        s_ref[...] = jnp.zeros_like(s_ref)

    for b in range(nseq):
        ab = ab_ref[b]
        sp = ab + dtb_ref[...]
        softplus = jnp.maximum(sp, 0.0) + jnp.log1p(jnp.exp(-jnp.abs(sp)))
        g = -jnp.exp(alog_ref[...]) * softplus
        pos = lax.broadcasted_iota(jnp.int32, g.shape, 0) % chunk
        step = 1
        while step < chunk:
            g = g + jnp.where(pos >= step, pltpu.roll(g, step, 0), 0.0)
            step *= 2
        gcat_s[b] = _expand(g, eg64_ref[...])
        gexp = _expand(g, eg128_ref[...])
        bexp = _expand(_sigmoid(ab), eb128_ref[...])
        eg = jnp.exp(gexp)
        eg_s[b] = eg
        g_last = jnp.concatenate(
            [jnp.broadcast_to(gexp[(c + 1) * chunk - 1:(c + 1) * chunk, :], (chunk, width))
             for c in range(nchunks)], axis=0)
        kd_scale = jnp.exp(g_last - gexp)

        for h in range(heads):
            hc = slice(h * hd, (h + 1) * hd)
            q = qkv_ref[b, :, hc]
            k = qkv_ref[b, :, width + h * hd:width + (h + 1) * hd]
            v = qkv_ref[b, :, 2 * width + h * hd:2 * width + (h + 1) * hd]
            kb = k * bexp[:, hc]
            q_s[b, :, hc] = q.astype(BF16)
            qd_s[b, :, hc] = (q * eg[:, hc]).astype(BF16)
            k_s[b, :, hc] = k.astype(BF16)
            kb_s[b, :, hc] = kb.astype(BF16)
            kd_s[b, :, hc] = (k * kd_scale[:, hc]).astype(BF16)
            vb_s[b, :, hc] = (v * bexp[:, hc]).astype(BF16)
            wr_s[b, :, hc] = (kb * eg[:, hc]).astype(BF16)

    ri = lax.broadcasted_iota(jnp.int32, (chunk, cat), 0)
    li = lax.broadcasted_iota(jnp.int32, (chunk, cat), 1) % chunk
    causal = ri >= li
    strict = ri > li
    diag = ri == li
    eye_cat = jnp.where(diag, 1.0, 0.0).astype(F32)
    rows = [slice(c * chunk, (c + 1) * chunk) for c in range(nchunks)]
    units = [(b, c) for c in range(nchunks) for b in range(nseq)]
    nunits = len(units)

    a_all, a_qk = [], []
    for b, c in units:
        k_bd = _block_diag(k_s[b, rows[c], :], hd)
        both = _dot_nt(jnp.concatenate([kb_s[b, rows[c], :], q_s[b, rows[c], :]], axis=0), k_bd)
        gc = gcat_s[b, rows[c], :]
        g_row = jnp.sum(jnp.where(diag, gc, 0.0), axis=0, keepdims=True)
        decay = jnp.where(causal, jnp.exp(gc - g_row), 0.0)
        a_low = jnp.where(strict, both[:chunk] * decay, 0.0)
        a_qk.append((both[chunk:] * decay).astype(BF16))
        a_all.append(a_low)
    same = lambda s: (ri // s) == (li // s)
    in_base = same(DN_INV_BASE)
    inv, pw = [], []
    for i in range(nunits):
        d_f = jnp.where(in_base, a_all[i], 0.0)
        d_b = d_f.astype(BF16)
        inv.append(eye_cat - d_f)
        pw.append(_dot(d_b, _block_diag(d_b, chunk)))
    n_base = int(math.log2(DN_INV_BASE)) - 1
    for it in range(n_base):
        for i in range(nunits):
            pw_b = pw[i].astype(BF16)
            pw_bd = _block_diag(pw_b, chunk)
            if it + 1 < n_base:
                r = _dot(jnp.concatenate([inv[i].astype(BF16), pw_b], axis=0), pw_bd)
                inv[i] = inv[i] + r[:chunk]
                pw[i] = r[chunk:]
            else:
                inv[i] = inv[i] + _dot(inv[i].astype(BF16), pw_bd)
    size = DN_INV_BASE
    while size < chunk:
        off_diag = same(2 * size) & jnp.logical_not(same(size))
        for i in range(nunits):
            t_b = inv[i].astype(BF16)
            n_b = jnp.where(off_diag, a_all[i], 0.0).astype(BF16)
            t_n = _dot(t_b, _block_diag(n_b, chunk))
            inv[i] = inv[i] - _dot(t_n.astype(BF16), _block_diag(t_b, chunk))
        size *= 2
    hcols = [slice(h * hd, (h + 1) * hd) for h in range(heads)]
    hrows = [slice(h * chunk, (h + 1) * chunk) for h in range(heads)]
    stack = lambda ref, b, c: jnp.concatenate(
        [ref[b, rows[c], hcols[h]] for h in range(heads)], axis=0)
    u, w = [], []
    for i, (b, c) in enumerate(units):
        rhs = jnp.concatenate([stack(vb_s, b, c), stack(wr_s, b, c)], axis=1)
        uw = _dot(_block_diag(inv[i].astype(BF16), chunk), rhs)
        u.append(uw[:, :hd])
        w.append(uw[:, hd:].astype(BF16))

    state = [[s_ref[b, h] for h in range(heads)] for b in range(nseq)]
    for i, (b, c) in enumerate(units):
        res = [_dot(jnp.concatenate([w[i][hrows[h], :], qd_s[b, rows[c], hcols[h]]], axis=0),
                    state[b][h].astype(BF16)) for h in range(heads)]
        v_new = u[i] - jnp.concatenate([r[:chunk] for r in res], axis=0)
        v_new_b = v_new.astype(BF16)
        o = (jnp.concatenate([r[chunk:] for r in res], axis=0)
             + _dot(_block_diag(a_qk[i], chunk), v_new_b))
        last = (c + 1) * chunk - 1
        for h in range(heads):
            o_ref[b, rows[c], hcols[h]] = o[hrows[h], :]
            upd = _dot_tn(kd_s[b, rows[c], hcols[h]], v_new_b[hrows[h], :])
            state[b][h] = state[b][h] * eg_s[b, last:last + 1, hcols[h]] + upd
    for b in range(nseq):
        for h in range(heads):
            s_ref[b, h] = state[b][h]

    for b in range(nseq):
        for h in range(heads):
            o = o_ref[b, :, hcols[h]]
            on = o * lax.rsqrt(jnp.mean(o * o, axis=-1, keepdims=True) + NORM_EPS) * onw_ref[...]
            o_ref[b, :, hcols[h]] = on * _silu(z_ref[b, :, hcols[h]])


def _deltanet(qkv, z, ab, alog_pad, dtb_pad, onw, tt, nseq):
    bsz, t, w3 = qkv.shape
    width = w3 // 3
    heads = width // DN_HEAD_DIM
    chunk = DN_CHUNK
    cat = heads * chunk
    hh = jnp.arange(AB_PAD)[:, None]
    e_g64 = (hh == jnp.arange(cat)[None, :] // chunk).astype(BF16)
    e_g128 = (hh == jnp.arange(width)[None, :] // DN_HEAD_DIM).astype(BF16)
    e_b128 = (hh == heads + jnp.arange(width)[None, :] // DN_HEAD_DIM).astype(BF16)
    row = lambda wd: pl.BlockSpec((nseq, tt, wd), lambda b, i: (b, i, 0))
    return pl.pallas_call(
        functools.partial(_dn_kernel, tt=tt, chunk=chunk, heads=heads),
        grid=(bsz // nseq, t // tt),
        in_specs=[row(w3), row(width), row(AB_PAD),
                  _resident(alog_pad.shape), _resident(dtb_pad.shape), _resident(onw.shape),
                  _resident(e_g64.shape), _resident(e_g128.shape), _resident(e_b128.shape)],
        out_specs=row(width),
        out_shape=jax.ShapeDtypeStruct((bsz, t, width), F32),
        scratch_shapes=[pltpu.VMEM((nseq, tt, width), BF16)] * 7
        + [pltpu.VMEM((nseq, tt, width), F32), pltpu.VMEM((nseq, tt, cat), F32),
           pltpu.VMEM((nseq, heads, DN_HEAD_DIM, DN_HEAD_DIM), F32)],
        compiler_params=_params(1, 1),
        name="deltanet",
    )(qkv, z, ab, alog_pad, dtb_pad, onw, e_g64, e_g128, e_b128)


def _gm_kernel(uv_ref, lng_ref, lnb_ref, ws_ref, bias_ref, o_ref, *, tt, groups):
    width = groups * GM_GROUP_DIM
    x = uv_ref[...]
    zz = 0.5 * x * (1.0 + lax.erf(x * (2.0 ** -0.5)))
    u, v = zz[:, :width], zz[:, width:]
    mu = jnp.mean(v, axis=-1, keepdims=True)
    vc = v - mu
    var = jnp.mean(vc * vc, axis=-1, keepdims=True)
    vn = (vc * lax.rsqrt(var + NORM_EPS) * lng_ref[...] + lnb_ref[...]).astype(BF16)
    ii = lax.broadcasted_iota(jnp.int32, (GM_CHUNK, GM_CHUNK), 0)
    jj = lax.broadcasted_iota(jnp.int32, (GM_CHUNK, GM_CHUNK), 1)
    ws = [jnp.where(ii >= jj, ws_ref[g], 0.0).astype(BF16) for g in range(groups)]
    bias = bias_ref[...]
    for c in range(tt // GM_CHUNK):
        rows = slice(c * GM_CHUNK, (c + 1) * GM_CHUNK)
        parts = [_dot(ws[g], vn[rows, g * GM_GROUP_DIM:(g + 1) * GM_GROUP_DIM])
                 for g in range(groups)]
        o_ref[rows, :] = u[rows, :] * (jnp.concatenate(parts, axis=1) + bias)


def _spatial_gating(uv, ln_g, ln_b, w_s, bias_full, tt):
    bsz, t, w2 = uv.shape
    width = w2 // 2
    groups = width // GM_GROUP_DIM
    return pl.pallas_call(
        functools.partial(_gm_kernel, tt=tt, groups=groups),
        grid=(bsz, t // tt),
        in_specs=[pl.BlockSpec((None, tt, w2), lambda b, i: (b, i, 0)),
                  _resident(ln_g.shape), _resident(ln_b.shape), _resident(w_s.shape),
                  _resident(bias_full.shape)],
        out_specs=pl.BlockSpec((None, tt, width), lambda b, i: (b, i, 0)),
        out_shape=jax.ShapeDtypeStruct((bsz, t, width), F32),
        compiler_params=_params(2),
        name="spatial_gating",
    )(uv, ln_g, ln_b, w_s, bias_full)


def _sw_kernel(q_ref, k_ref, v_ref, kp_ref, vp_ref, o_ref, lse_ref, *, dilation, heads, span):
    blk = SW_BLOCK
    hd = SW_HEAD_DIM
    nk = 2 * blk
    nqb = q_ref.shape[1] // blk
    assert 2 * hd == LANES and heads % 2 == 0
    a = lax.broadcasted_iota(jnp.int32, (blk, 2 * nk), 0)
    c = lax.broadcasted_iota(jnp.int32, (blk, 2 * nk), 1) % nk
    dist = blk + a - c
    in_window = (dist >= 0) & (dist <= span)
    own_block = c >= blk
    ones_bd = (lax.broadcasted_iota(jnp.int32, (2 * nk, LANES), 0) // nk
               == lax.broadcasted_iota(jnp.int32, (2 * nk, LANES), 1) // hd).astype(BF16)
    units = [(r, qb, j) for r in range(dilation) for qb in range(nqb) for j in range(heads // 2)]
    group = 4
    for g0 in range(0, len(units), group):
        grp = units[g0:g0 + group]
        ss, vos = [], []
        for r, qb, j in grp:
            rows = slice(qb * blk, (qb + 1) * blk)
            lanes = slice(j * LANES, (j + 1) * LANES)
            if qb == 0:
                kk = jnp.concatenate([kp_ref[r, :, lanes], k_ref[r, rows, lanes]], axis=0)
                vv = jnp.concatenate([vp_ref[r, :, lanes], v_ref[r, rows, lanes]], axis=0)
            else:
                kk = k_ref[r, (qb - 1) * blk:(qb + 1) * blk, lanes]
                vv = v_ref[r, (qb - 1) * blk:(qb + 1) * blk, lanes]
            s = _dot_nt(q_ref[r, rows, lanes], _block_diag(kk, hd))
            has_prev = (pl.program_id(1) * nqb + qb) > 0
            ss.append(jnp.where(in_window & (own_block | has_prev), s, -jnp.inf))
            vos.append(jnp.concatenate([_block_diag(vv, hd), ones_bd], axis=1))
        ms, ps = [], []
        for s in ss:
            m_a = jnp.max(s[:, :nk], axis=-1, keepdims=True)
            m_b = jnp.max(s[:, nk:], axis=-1, keepdims=True)
            m_wide = jnp.concatenate([jnp.broadcast_to(m_a, (blk, nk)),
                                      jnp.broadcast_to(m_b, (blk, nk))], axis=1)
            ps.append(jnp.exp(s - m_wide).astype(BF16))
            ms.append(jnp.concatenate([jnp.broadcast_to(m_a, (blk, hd)),
                                       jnp.broadcast_to(m_b, (blk, hd))], axis=1))
        for (r, qb, j), p, vo, m in zip(grp, ps, vos, ms):
            pv = _dot(p, vo)
            l = pv[:, LANES:]
            if dilation == 1:
                tok = slice(qb * blk, (qb + 1) * blk)
            else:
                tok = pl.ds(r + dilation * qb * blk, blk, stride=dilation)
            o_ref[j, tok, :] = pv[:, :LANES] / l
            lse_ref[j, tok, :] = m + jnp.log(l)


def _window_attention(qkv, window, dilation):
    bsz, _, length, w3 = qkv.shape
    t = length * dilation
    width = w3 // 3
    heads = width // SW_HEAD_DIM
    span = window // dilation
    assert span <= SW_BLOCK and length % SW_BLOCK == 0
    nb = max(1, 8 // dilation)
    rl = nb * SW_BLOCK
    cur = lambda j: pl.BlockSpec((None, dilation, rl, width), lambda b, i: (b, 0, i, j))
    prev = lambda j: pl.BlockSpec((None, dilation, SW_BLOCK, width),
                                  lambda b, i: (b, 0, jnp.maximum(i * nb - 1, 0), j))
    out = pl.BlockSpec((None, width // LANES, rl * dilation, LANES), lambda b, i: (b, 0, i, 0))
    return pl.pallas_call(
        functools.partial(_sw_kernel, dilation=dilation, heads=heads, span=span),
        grid=(bsz, length // rl),
        in_specs=[cur(0), cur(1), cur(2), prev(1), prev(2)],
        out_specs=[out, out],
        out_shape=[jax.ShapeDtypeStruct((bsz, width // LANES, t, LANES), F32)] * 2,
        compiler_params=_params(2),
        name=f"window_attention_d{dilation}",
    )(qkv, qkv, qkv, qkv, qkv)


def _out_kernel(x_ref, ya_ref, yb_ref, o1_ref, l1_ref, o2_ref, l2_ref, o3_ref, l3_ref,
                gate1_ref, shift2_ref, scale2_ref, gate2_ref, fnw_ref,
                wout_ref, wfi_ref, wfo_ref, out_ref, *, ffn_chunks):
    ycs = []
    for j in range(o1_ref.shape[0]):
        l1, l2, l3 = l1_ref[j], l2_ref[j], l3_ref[j]
        m = jnp.maximum(jnp.maximum(l1, l2), l3)
        e1, e2, e3 = jnp.exp(l1 - m), jnp.exp(l2 - m), jnp.exp(l3 - m)
        yc = (e1 * o1_ref[j] + e2 * o2_ref[j] + e3 * o3_ref[j]) / (e1 + e2 + e3)
        ycs.append(yc.astype(BF16))
    y = jnp.concatenate([ya_ref[...].astype(BF16), yb_ref[...].astype(BF16)] + ycs, axis=1)
    x1 = x_ref[...] + gate1_ref[...] * _dot(y, wout_ref[...])
    ms = jnp.mean(x1 * x1, axis=-1, keepdims=True)
    h = x1 * lax.rsqrt(ms + NORM_EPS) * fnw_ref[...] * (1.0 + scale2_ref[...]) + shift2_ref[...]
    hb = h.astype(BF16)
    hidden = wfo_ref.shape[0]
    fc = hidden // ffn_chunks
    acc = None
    for j in range(ffn_chunks):
        gate = _dot(hb, wfi_ref[:, j * fc:(j + 1) * fc])
        up = _dot(hb, wfi_ref[:, hidden + j * fc:hidden + (j + 1) * fc])
        part = _dot((_silu(gate) * up).astype(BF16), wfo_ref[j * fc:(j + 1) * fc, :])
        acc = part if acc is None else acc + part
    out_ref[...] = x1 + gate2_ref[...] * acc


def _out_ffn(x, ya, yb, attn, mod4, fnw, wout, wfi, wfo, tm):
    bsz, t, d = x.shape
    row = lambda wd: pl.BlockSpec((None, tm, wd), lambda b, i: (b, i, 0))
    modspec = lambda k: pl.BlockSpec((None, None, 1, d), lambda b, i: (b, k, 0, 0))
    attn_flat = [a for pair in attn for a in pair]
    return pl.pallas_call(
        functools.partial(_out_kernel, ffn_chunks=2),
        grid=(bsz, t // tm),
        in_specs=[row(d), row(ya.shape[-1]), row(yb.shape[-1])]
        + [pl.BlockSpec((None, a.shape[1], tm, LANES), lambda b, i: (b, 0, i, 0)) for a in attn_flat]
        + [modspec(2), modspec(3), modspec(4), modspec(5), _resident(fnw.shape),
           _resident(wout.shape), _resident(wfi.shape), _resident(wfo.shape)],
        out_specs=row(d),
        out_shape=jax.ShapeDtypeStruct((bsz, t, d), F32),
        compiler_params=_params(2),
        name="out_ffn",
    )(x, ya, yb, *attn_flat, mod4, mod4, mod4, mod4, fnw, wout, wfi, wfo)


def _rotary_tables(t, heads):
    half = ROPE_DIM // 2
    inv = ROPE_THETA ** (-jnp.arange(0, ROPE_DIM, 2, dtype=F32) / ROPE_DIM)
    ang = jnp.arange(t, dtype=F32)[:, None] * inv[None, :]
    cos, sin = jnp.cos(ang), jnp.sin(ang)
    rest = SW_HEAD_DIM - ROPE_DIM
    ones = jnp.ones((t, rest), F32)
    zeros = jnp.zeros((t, rest), F32)
    zh = jnp.zeros((t, half), F32)
    cos_f = jnp.concatenate([cos, cos, ones], axis=1)
    sin_a = jnp.concatenate([-sin, zh, zeros], axis=1)
    sin_b = jnp.concatenate([zh, sin, zeros], axis=1)
    tile = lambda a: jnp.tile(a, (1, heads))
    return tile(cos_f), tile(sin_a), tile(sin_b)


def kernel(x, c, w_mod, b_mod, mix_norm_w, ffn_norm_w, w_in, w_out, dn_conv_w, dn_a_log,
           dn_dt_bias, dn_out_norm_w, gm_ln_g, gm_ln_b, gm_w_s, gm_b_s, sw_q_norm_w,
           sw_k_norm_w, w_ffn_in, w_ffn_out):
    bsz, t, d = x.shape
    nl = w_mod.shape[0]
    dn_heads = dn_a_log.shape[1]
    dn_width = dn_heads * DN_HEAD_DIM
    gm_groups = gm_w_s.shape[1]
    gm_width = gm_groups * GM_GROUP_DIM
    npat = len(SW_PATTERNS)
    sw_width = (w_in.shape[2] - 4 * dn_width - 2 * dn_heads - 2 * gm_width) // (3 * npat)
    sw_heads = sw_width // SW_HEAD_DIM
    assert 2 * dn_heads <= AB_PAD

    o_z = 3 * dn_width
    o_a = o_z + dn_width
    o_gm = o_a + 2 * dn_heads
    o_sw = o_gm + 2 * gm_width
    widths = (3 * dn_width, dn_width, 2 * gm_width) + (3 * sw_width,) * npat + (AB_PAD,)

    mod = _modulation(c, w_mod, b_mod)
    tables = _rotary_tables(t, sw_heads)
    ii = jnp.arange(sw_width) // SW_HEAD_DIM
    ones_bd = (ii[:, None] == ii[None, :]).astype(BF16)
    tm = 512

    for layer in range(nl):
        wl = w_in[layer]
        ab_cols = jnp.zeros((d, AB_PAD), F32).at[:, :2 * dn_heads].set(wl[:, o_a:o_gm])
        w_cat = jnp.concatenate(
            [wl[:, :o_a], wl[:, o_gm:o_sw], wl[:, o_sw:], ab_cols], axis=1).astype(BF16)
        mod4 = mod[layer].reshape(bsz, 6, 1, d)
        dn_qkv, dn_z, gm_uv, sw1, sw2, sw3, dn_ab = _in_projection(
            x, mod4, mix_norm_w[layer].reshape(1, d), w_cat, tables,
            jnp.tile(sw_q_norm_w[layer], sw_heads).reshape(1, sw_width),
            jnp.tile(sw_k_norm_w[layer], sw_heads).reshape(1, sw_width),
            ones_bd, dn_conv_w[layer], widths, tm)

        pad_heads = lambda v: jnp.zeros((1, AB_PAD), F32).at[0, :dn_heads].set(v)
        y_a = _deltanet(dn_qkv, dn_z, dn_ab, pad_heads(dn_a_log[layer]),
                        pad_heads(dn_dt_bias[layer]), dn_out_norm_w[layer].reshape(1, DN_HEAD_DIM),
                        tt=512, nseq=2 if bsz % 2 == 0 else 1)
        bias_full = jnp.repeat(gm_b_s[layer].T, GM_GROUP_DIM, axis=1)
        y_b = _spatial_gating(gm_uv, gm_ln_g[layer].reshape(1, gm_width),
                              gm_ln_b[layer].reshape(1, gm_width), gm_w_s[layer], bias_full,
                              tt=512)
        attn = [_window_attention(sw, window, dilation)
                for sw, (window, dilation) in zip((sw1, sw2, sw3), SW_PATTERNS)]
        x = _out_ffn(x, y_a, y_b, attn, mod4, ffn_norm_w[layer].reshape(1, d),
                     w_out[layer].astype(BF16), w_ffn_in[layer].astype(BF16),
                     w_ffn_out[layer].astype(BF16), tm)
    return x
```

```python
import functools
import math

import jax
import jax.numpy as jnp
from jax import lax
from jax.experimental import pallas as pl
from jax.experimental.pallas import tpu as pltpu

F32 = jnp.float32
BF16 = jnp.bfloat16
HIGHEST = lax.Precision.HIGHEST

NORM_EPS = 1e-6
DN_HEAD_DIM = 128
DN_CONV = 4
DN_CHUNK = 64
DN_INV_BASE = 16
GM_GROUP_DIM = 64
GM_CHUNK = 128
SW_HEAD_DIM = 64
SW_PATTERNS = ((128, 1), (512, 4), (2048, 16))
SW_BLOCK = 128
ROPE_THETA = 500000.0
ROPE_DIM = SW_HEAD_DIM // 4
LANES = 128
AB_PAD = LANES

VMEM_LIMIT_BYTES = 56 * 1024 * 1024


def _params(n_parallel, n_arbitrary=0):
    return pltpu.CompilerParams(
        dimension_semantics=("parallel",) * n_parallel + ("arbitrary",) * n_arbitrary,
        vmem_limit_bytes=VMEM_LIMIT_BYTES,
    )


def _resident(shape):
    zeros = (0,) * len(shape)
    return pl.BlockSpec(shape, lambda *_: zeros, pipeline_mode=pl.Buffered(1))


def _layer_resident(stacked, layer):
    zeros = (0,) * (stacked.ndim - 1)
    return pl.BlockSpec((None,) + stacked.shape[1:], lambda *_: (layer,) + zeros,
                        pipeline_mode=pl.Buffered(1))


def _sigmoid(x):
    return 1.0 / (1.0 + jnp.exp(-x))


def _silu(x):
    return x * _sigmoid(x)


def _dot(a, b):
    return jnp.dot(a, b, preferred_element_type=F32)


def _dot_nt(a, b):
    return lax.dot_general(a, b, (((1,), (1,)), ((), ())), preferred_element_type=F32)


def _dot_tn(a, b):
    return lax.dot_general(a, b, (((0,), (0,)), ((), ())), preferred_element_type=F32)


def _dot_hi(a, b):
    return jnp.dot(a, b, preferred_element_type=F32, precision=HIGHEST)


def _mod_kernel(c_ref, w_ref, b_ref, o_ref):
    c = c_ref[...]
    o_ref[...] = _dot_hi(_silu(c), w_ref[...]) + b_ref[...]


def _modulation(c, w_mod, b_mod):
    nl, d, e = w_mod.shape
    bsz = c.shape[0]
    tn = d
    return pl.pallas_call(
        _mod_kernel,
        grid=(nl, e // tn),
        in_specs=[
            pl.BlockSpec((bsz, d), lambda l, j: (0, 0)),
            pl.BlockSpec((None, d, tn), lambda l, j: (l, 0, j)),
            pl.BlockSpec((None, 1, tn), lambda l, j: (l, 0, j)),
        ],
        out_specs=pl.BlockSpec((None, bsz, tn), lambda l, j: (l, 0, j)),
        out_shape=jax.ShapeDtypeStruct((nl, bsz, e), F32),
        compiler_params=_params(2),
        name="modulation",
    )(c, w_mod, b_mod.reshape(nl, 1, e))


def _group_mean_sq(x, ones_bd, group):
    del group
    return _dot((x * x).astype(BF16), ones_bd)


def _rotary(x, cos_f, sin_a, sin_b):
    n = x.shape[-1]
    half = ROPE_DIM // 2
    return x * cos_f + pltpu.roll(x, n - half, 1) * sin_a + pltpu.roll(x, half, 1) * sin_b


def _in_kernel(x_ref, shift_ref, scale_ref, nw_ref, w_ref, cos_ref, sina_ref, sinb_ref,
               qnw_ref, knw_ref, bd_ref, cw_ref,
               dn_ref, z_ref, gm_ref, sw1_ref, sw2_ref, sw3_ref, ab_ref, stage_ref, hist_ref,
               *, widths):
    tm = x_ref.shape[0]
    x = x_ref[...]
    ms = jnp.mean(x * x, axis=-1, keepdims=True)
    h = x * lax.rsqrt(ms + NORM_EPS) * (nw_ref[...] * (1.0 + scale_ref[...])) + shift_ref[...]
    hb = h.astype(BF16)
    offs = [0]
    for wd in widths:
        offs.append(offs[-1] + wd)
    plain = ((z_ref, 1), (gm_ref, 2), (ab_ref, 6))
    for ref, idx in plain:
        ref[...] = _dot(hb, w_ref[:, offs[idx]:offs[idx + 1]])

    pad = hist_ref.shape[1] - tm
    nblk = hist_ref.shape[0]

    @pl.when(pl.program_id(1) == 0)
    def _():
        hist_ref[:, 0:pad, :] = jnp.zeros((nblk, pad, LANES), F32)

    @pl.when(pl.program_id(1) > 0)
    def _():
        hist_ref[:, 0:pad, :] = hist_ref[:, tm:tm + pad, :]

    raw = _dot(hb, w_ref[:, offs[0]:offs[1]])
    for cb in range(nblk):
        hist_ref[cb, pad:pad + tm, :] = raw[:, cb * LANES:(cb + 1) * LANES]
    phases = 8
    per = tm // phases
    for cb in range(nblk):
        cols = slice(cb * LANES, (cb + 1) * LANES)
        taps = [cw_ref[tap:tap + 1, cols] for tap in range(DN_CONV)]
        x = {j: hist_ref[cb, pl.ds(pad + j, per, stride=phases), :]
             for j in range(1 - DN_CONV, phases)}
        for s in range(phases):
            y = taps[DN_CONV - 1] * x[s]
            for tap in range(DN_CONV - 1):
                y = y + taps[tap] * x[s - (DN_CONV - 1 - tap)]
            y = _silu(y)
            if cb < 2 * nblk // 3:
                y = y * lax.rsqrt(jnp.sum(y * y, axis=-1, keepdims=True) + NORM_EPS)
            if cb < nblk // 3:
                y = y * (DN_HEAD_DIM ** -0.5)
            dn_ref[cb, pl.ds(s, per, stride=phases), :] = y
    sww = widths[3] // 3
    cos_f, sin_a, sin_b = cos_ref[...], sina_ref[...], sinb_ref[...]
    bd = bd_ref[...]
    for ref, idx in ((sw1_ref, 3), (sw2_ref, 4), (sw3_ref, 5)):
        raw = _dot(hb, w_ref[:, offs[idx]:offs[idx + 1]])
        q, k, v = raw[:, :sww], raw[:, sww:2 * sww], raw[:, 2 * sww:]
        qn = q * lax.rsqrt(_group_mean_sq(q, bd, SW_HEAD_DIM) + NORM_EPS) * qnw_ref[...]
        kn = k * lax.rsqrt(_group_mean_sq(k, bd, SW_HEAD_DIM) + NORM_EPS) * knw_ref[...]
        qn = _rotary(qn, cos_f, sin_a, sin_b)
        kn = _rotary(kn, cos_f, sin_a, sin_b)
        dilation = ref.shape[0]
        if dilation == 1:
            ref[0, :, :sww] = qn.astype(BF16)
            ref[0, :, sww:2 * sww] = kn.astype(BF16)
            ref[0, :, 2 * sww:] = v.astype(BF16)
        else:
            parts = (qn, kn, v)
            per = sww // LANES
            for j in range(3 * per):
                stage_ref[j] = parts[j // per][:, (j % per) * LANES:(j % per + 1) * LANES]
            for r in range(dilation):
                for j in range(3 * per):
                    ref[r, :, j * LANES:(j + 1) * LANES] = stage_ref[
                        j, pl.ds(r, tm // dilation, stride=dilation), :].astype(BF16)


def _in_projection(x, mod4, nw, w_cat, layer, tables, qnw, knw, ones_bd, conv_w, widths, tm):
    bsz, t, d = x.shape
    cos_f, sin_a, sin_b = tables
    sww = widths[3] // 3
    row = lambda wd: pl.BlockSpec((None, tm, wd), lambda b, i: (b, i, 0))
    modspec = lambda k: pl.BlockSpec((None, None, 1, d), lambda b, i: (b, k, 0, 0))
    tabspec = pl.BlockSpec((tm, sww), lambda b, i: (i, 0))
    sw_specs = [pl.BlockSpec((None, dil, tm // dil, 3 * sww), lambda b, i: (b, 0, i, 0))
                for _, dil in SW_PATTERNS]
    sw_shapes = [jax.ShapeDtypeStruct((bsz, dil, t // dil, 3 * sww), BF16) for _, dil in SW_PATTERNS]
    f32_out = lambda wd: jax.ShapeDtypeStruct((bsz, t, wd), F32)
    return pl.pallas_call(
        functools.partial(_in_kernel, widths=widths),
        grid=(bsz, t // tm),
        in_specs=[
            row(d), modspec(0), modspec(1), _resident(nw.shape), _layer_resident(w_cat, layer),
            tabspec, tabspec, tabspec, _resident(qnw.shape), _resident(knw.shape),
            _resident(ones_bd.shape), _resident(conv_w.shape),
        ],
        out_specs=[pl.BlockSpec((None, widths[0] // LANES, tm, LANES), lambda b, i: (b, 0, i, 0)),
                   row(widths[1]), row(widths[2])] + sw_specs + [row(widths[6])],
        out_shape=[jax.ShapeDtypeStruct((bsz, widths[0] // LANES, t, LANES), F32),
                   f32_out(widths[1]), f32_out(widths[2])] + sw_shapes
        + [f32_out(widths[6])],
        scratch_shapes=[pltpu.VMEM((3 * sww // LANES, tm, LANES), F32),
                        pltpu.VMEM((widths[0] // LANES, tm + 8, LANES), F32)],
        compiler_params=_params(1, 1),
        name="in_projection",
    )(x, mod4, mod4, nw, w_cat, cos_f, sin_a, sin_b, qnw, knw, ones_bd, conv_w)


def _expand(x, e):
    x1 = x.astype(BF16)
    r = x - x1.astype(F32)
    x2 = r.astype(BF16)
    x3 = (r - x2.astype(F32)).astype(BF16)
    return _dot(x1, e) + _dot(x2, e) + _dot(x3, e)


def _block_diag(x, blk):
    c, n = x.shape
    zero = jnp.zeros((c, LANES), x.dtype)
    low = lax.broadcasted_iota(jnp.int32, (c, LANES), 1) < blk
    out = []
    for h in range(n // blk):
        g = (h * blk) // LANES
        piece = x[:, g * LANES:(g + 1) * LANES]
        if blk < LANES:
            piece = jnp.where(low if (h * blk) % LANES == 0 else jnp.logical_not(low), piece, zero)
        out.append(jnp.concatenate([piece if j == g else zero for j in range(n // LANES)], axis=1))
    return jnp.concatenate(out, axis=0)


def _dn_kernel(qkv_ref, z_ref, ab_ref, alog_ref, dtb_ref, onw_ref, eg64_ref, eg128_ref,
               eb128_ref, o_ref, q_s, qd_s, k_s, kb_s, kd_s, vb_s, wr_s, eg_s, gcat_s, s_ref,
               *, tt, chunk, heads):
    hd = DN_HEAD_DIM
    nseq = qkv_ref.shape[0]
    width = heads * hd
    cat = heads * chunk
    nchunks = tt // chunk
    assert hd == LANES and 2 * chunk == LANES

    @pl.when(pl.program_id(1) == 0)
    def _():
        s_ref[...] = jnp.zeros_like(s_ref)

    for b in range(nseq):
        ab = ab_ref[b]
        sp = ab + dtb_ref[...]
        softplus = jnp.maximum(sp, 0.0) + jnp.log1p(jnp.exp(-jnp.abs(sp)))
        g = -jnp.exp(alog_ref[...]) * softplus
        pos = lax.broadcasted_iota(jnp.int32, g.shape, 0) % chunk
        step = 1
        while step < chunk:
            g = g + jnp.where(pos >= step, pltpu.roll(g, step, 0), 0.0)
            step *= 2
        gcat_s[b] = _expand(g, eg64_ref[...])
        gexp = _expand(g, eg128_ref[...])
        bexp = _expand(_sigmoid(ab), eb128_ref[...])
        eg = jnp.exp(gexp)
        eg_s[b] = eg
        g_last = jnp.concatenate(
            [jnp.broadcast_to(gexp[(c + 1) * chunk - 1:(c + 1) * chunk, :], (chunk, width))
             for c in range(nchunks)], axis=0)
        kd_scale = jnp.exp(g_last - gexp)

        for h in range(heads):
            hc = slice(h * hd, (h + 1) * hd)
            q, k, v = qkv_ref[b, h], qkv_ref[b, heads + h], qkv_ref[b, 2 * heads + h]
            kb = k * bexp[:, hc]
            q_s[b, :, hc] = q.astype(BF16)
            qd_s[b, :, hc] = (q * eg[:, hc]).astype(BF16)
            k_s[b, :, hc] = k.astype(BF16)
            kb_s[b, :, hc] = kb.astype(BF16)
            kd_s[b, :, hc] = (k * kd_scale[:, hc]).astype(BF16)
            vb_s[b, :, hc] = (v * bexp[:, hc]).astype(BF16)
            wr_s[b, :, hc] = (kb * eg[:, hc]).astype(BF16)

    ri = lax.broadcasted_iota(jnp.int32, (chunk, cat), 0)
    li = lax.broadcasted_iota(jnp.int32, (chunk, cat), 1) % chunk
    causal = ri >= li
    strict = ri > li
    diag = ri == li
    eye_cat = jnp.where(diag, 1.0, 0.0).astype(F32)
    rows = [slice(c * chunk, (c + 1) * chunk) for c in range(nchunks)]
    units = [(b, c) for c in range(nchunks) for b in range(nseq)]
    nunits = len(units)

    a_all, a_qk = [], []
    for b, c in units:
        k_bd = _block_diag(k_s[b, rows[c], :], hd)
        both = _dot_nt(jnp.concatenate([kb_s[b, rows[c], :], q_s[b, rows[c], :]], axis=0), k_bd)
        gc = gcat_s[b, rows[c], :]
        g_row = jnp.sum(jnp.where(diag, gc, 0.0), axis=0, keepdims=True)
        decay = jnp.where(causal, jnp.exp(gc - g_row), 0.0)
        a_low = jnp.where(strict, both[:chunk] * decay, 0.0)
        a_qk.append((both[chunk:] * decay).astype(BF16))
        a_all.append(a_low)
    same = lambda s: (ri // s) == (li // s)
    in_base = same(DN_INV_BASE)
    inv, pw = [], []
    for i in range(nunits):
        d_f = jnp.where(in_base, a_all[i], 0.0)
        d_b = d_f.astype(BF16)
        inv.append(eye_cat - d_f)
        pw.append(_dot(d_b, _block_diag(d_b, chunk)))
    n_base = int(math.log2(DN_INV_BASE)) - 1
    for it in range(n_base):
        for i in range(nunits):
            pw_b = pw[i].astype(BF16)
            pw_bd = _block_diag(pw_b, chunk)
            if it + 1 < n_base:
                r = _dot(jnp.concatenate([inv[i].astype(BF16), pw_b], axis=0), pw_bd)
                inv[i] = inv[i] + r[:chunk]
                pw[i] = r[chunk:]
            else:
                inv[i] = inv[i] + _dot(inv[i].astype(BF16), pw_bd)
    size = DN_INV_BASE
    while size < chunk:
        off_diag = same(2 * size) & jnp.logical_not(same(size))
        for i in range(nunits):
            t_b = inv[i].astype(BF16)
            n_b = jnp.where(off_diag, a_all[i], 0.0).astype(BF16)
            t_n = _dot(t_b, _block_diag(n_b, chunk))
            inv[i] = inv[i] - _dot(t_n.astype(BF16), _block_diag(t_b, chunk))
        size *= 2
    hcols = [slice(h * hd, (h + 1) * hd) for h in range(heads)]
    hrows = [slice(h * chunk, (h + 1) * chunk) for h in range(heads)]
    stack = lambda ref, b, c: jnp.concatenate(
        [ref[b, rows[c], hcols[h]] for h in range(heads)], axis=0)
    u, w = [], []
    for i, (b, c) in enumerate(units):
        rhs = jnp.concatenate([stack(vb_s, b, c), stack(wr_s, b, c)], axis=1)
        uw = _dot(_block_diag(inv[i].astype(BF16), chunk), rhs)
        u.append(uw[:, :hd])
        w.append(uw[:, hd:].astype(BF16))

    state = [[s_ref[b, h] for h in range(heads)] for b in range(nseq)]
    for i, (b, c) in enumerate(units):
        res = [_dot(jnp.concatenate([w[i][hrows[h], :], qd_s[b, rows[c], hcols[h]]], axis=0),
                    state[b][h].astype(BF16)) for h in range(heads)]
        v_new = u[i] - jnp.concatenate([r[:chunk] for r in res], axis=0)
        v_new_b = v_new.astype(BF16)
        o = (jnp.concatenate([r[chunk:] for r in res], axis=0)
             + _dot(_block_diag(a_qk[i], chunk), v_new_b))
        last = (c + 1) * chunk - 1
        for h in range(heads):
            o_ref[b, rows[c], hcols[h]] = o[hrows[h], :]
            upd = _dot_tn(kd_s[b, rows[c], hcols[h]], v_new_b[hrows[h], :])
            state[b][h] = state[b][h] * eg_s[b, last:last + 1, hcols[h]] + upd
    for b in range(nseq):
        for h in range(heads):
            s_ref[b, h] = state[b][h]

    for b in range(nseq):
        for h in range(heads):
            o = o_ref[b, :, hcols[h]]
            on = o * lax.rsqrt(jnp.mean(o * o, axis=-1, keepdims=True) + NORM_EPS) * onw_ref[...]
            o_ref[b, :, hcols[h]] = on * _silu(z_ref[b, :, hcols[h]])


def _deltanet(qkv, z, ab, alog_pad, dtb_pad, onw, tt, nseq):
    bsz, nblk, t, _ = qkv.shape
    heads = nblk // 3
    width = heads * DN_HEAD_DIM
    chunk = DN_CHUNK
    cat = heads * chunk
    hh = jnp.arange(AB_PAD)[:, None]
    e_g64 = (hh == jnp.arange(cat)[None, :] // chunk).astype(BF16)
    e_g128 = (hh == jnp.arange(width)[None, :] // DN_HEAD_DIM).astype(BF16)
    e_b128 = (hh == heads + jnp.arange(width)[None, :] // DN_HEAD_DIM).astype(BF16)
    row = lambda wd: pl.BlockSpec((nseq, tt, wd), lambda b, i: (b, i, 0))
    return pl.pallas_call(
        functools.partial(_dn_kernel, tt=tt, chunk=chunk, heads=heads),
        grid=(bsz // nseq, t // tt),
        in_specs=[pl.BlockSpec((nseq, nblk, tt, LANES), lambda b, i: (b, 0, i, 0)),
                  row(width), row(AB_PAD),
                  _resident(alog_pad.shape), _resident(dtb_pad.shape), _resident(onw.shape),
                  _resident(e_g64.shape), _resident(e_g128.shape), _resident(e_b128.shape)],
        out_specs=row(width),
        out_shape=jax.ShapeDtypeStruct((bsz, t, width), F32),
        scratch_shapes=[pltpu.VMEM((nseq, tt, width), BF16)] * 7
        + [pltpu.VMEM((nseq, tt, width), F32), pltpu.VMEM((nseq, tt, cat), F32),
           pltpu.VMEM((nseq, heads, DN_HEAD_DIM, DN_HEAD_DIM), F32)],
        compiler_params=_params(1, 1),
        name="deltanet",
    )(qkv, z, ab, alog_pad, dtb_pad, onw, e_g64, e_g128, e_b128)


def _gm_kernel(uv_ref, lng_ref, lnb_ref, ws_ref, bias_ref, o_ref, *, tt, groups):
    width = groups * GM_GROUP_DIM
    x = uv_ref[...]
    zz = 0.5 * x * (1.0 + lax.erf(x * (2.0 ** -0.5)))
    u, v = zz[:, :width], zz[:, width:]
    mu = jnp.mean(v, axis=-1, keepdims=True)
    vc = v - mu
    var = jnp.mean(vc * vc, axis=-1, keepdims=True)
    vn = (vc * lax.rsqrt(var + NORM_EPS) * lng_ref[...] + lnb_ref[...]).astype(BF16)
    ii = lax.broadcasted_iota(jnp.int32, (GM_CHUNK, GM_CHUNK), 0)
    jj = lax.broadcasted_iota(jnp.int32, (GM_CHUNK, GM_CHUNK), 1)
    ws = [jnp.where(ii >= jj, ws_ref[g], 0.0).astype(BF16) for g in range(groups)]
    bias = bias_ref[...]
    for c in range(tt // GM_CHUNK):
        rows = slice(c * GM_CHUNK, (c + 1) * GM_CHUNK)
        parts = [_dot(ws[g], vn[rows, g * GM_GROUP_DIM:(g + 1) * GM_GROUP_DIM])
                 for g in range(groups)]
        o_ref[rows, :] = u[rows, :] * (jnp.concatenate(parts, axis=1) + bias)


def _spatial_gating(uv, ln_g, ln_b, w_s, bias_full, tt):
    bsz, t, w2 = uv.shape
    width = w2 // 2
    groups = width // GM_GROUP_DIM
    return pl.pallas_call(
        functools.partial(_gm_kernel, tt=tt, groups=groups),
        grid=(bsz, t // tt),
        in_specs=[pl.BlockSpec((None, tt, w2), lambda b, i: (b, i, 0)),
                  _resident(ln_g.shape), _resident(ln_b.shape), _resident(w_s.shape),
                  _resident(bias_full.shape)],
        out_specs=pl.BlockSpec((None, tt, width), lambda b, i: (b, i, 0)),
        out_shape=jax.ShapeDtypeStruct((bsz, t, width), F32),
        compiler_params=_params(2),
        name="spatial_gating",
    )(uv, ln_g, ln_b, w_s, bias_full)


def _sw_kernel(q_ref, k_ref, v_ref, kp_ref, vp_ref, o_ref, lse_ref, *, dilation, heads, span):
    blk = SW_BLOCK
    hd = SW_HEAD_DIM
    nk = 2 * blk
    nqb = q_ref.shape[1] // blk
    assert 2 * hd == LANES and heads % 2 == 0
    a = lax.broadcasted_iota(jnp.int32, (blk, 2 * nk), 0)
    c = lax.broadcasted_iota(jnp.int32, (blk, 2 * nk), 1) % nk
    dist = blk + a - c
    in_window = (dist >= 0) & (dist <= span)
    own_block = c >= blk
    ones_bd = (lax.broadcasted_iota(jnp.int32, (2 * nk, LANES), 0) // nk
               == lax.broadcasted_iota(jnp.int32, (2 * nk, LANES), 1) // hd).astype(BF16)
    units = [(r, qb, j) for r in range(dilation) for qb in range(nqb) for j in range(heads // 2)]
    group = 4
    for g0 in range(0, len(units), group):
        grp = units[g0:g0 + group]
        ss, vos = [], []
        for r, qb, j in grp:
            rows = slice(qb * blk, (qb + 1) * blk)
            lanes = slice(j * LANES, (j + 1) * LANES)
            if qb == 0:
                kk = jnp.concatenate([kp_ref[r, :, lanes], k_ref[r, rows, lanes]], axis=0)
                vv = jnp.concatenate([vp_ref[r, :, lanes], v_ref[r, rows, lanes]], axis=0)
            else:
                kk = k_ref[r, (qb - 1) * blk:(qb + 1) * blk, lanes]
                vv = v_ref[r, (qb - 1) * blk:(qb + 1) * blk, lanes]
            s = _dot_nt(q_ref[r, rows, lanes], _block_diag(kk, hd))
            has_prev = (pl.program_id(1) * nqb + qb) > 0
            ss.append(jnp.where(in_window & (own_block | has_prev), s, -jnp.inf))
            vos.append(jnp.concatenate([_block_diag(vv, hd), ones_bd], axis=1))
        ms, ps = [], []
        for s in ss:
            m_a = jnp.max(s[:, :nk], axis=-1, keepdims=True)
            m_b = jnp.max(s[:, nk:], axis=-1, keepdims=True)
            m_wide = jnp.concatenate([jnp.broadcast_to(m_a, (blk, nk)),
                                      jnp.broadcast_to(m_b, (blk, nk))], axis=1)
            ps.append(jnp.exp(s - m_wide).astype(BF16))
            ms.append(jnp.concatenate([jnp.broadcast_to(m_a, (blk, hd)),
                                       jnp.broadcast_to(m_b, (blk, hd))], axis=1))
        for (r, qb, j), p, vo, m in zip(grp, ps, vos, ms):
            pv = _dot(p, vo)
            l = pv[:, LANES:]
            if dilation == 1:
                tok = slice(qb * blk, (qb + 1) * blk)
            else:
                tok = pl.ds(r + dilation * qb * blk, blk, stride=dilation)
            o_ref[j, tok, :] = pv[:, :LANES] / l
            lse_ref[j, tok, :] = m + jnp.log(l)


def _window_attention(qkv, window, dilation):
    bsz, _, length, w3 = qkv.shape
    t = length * dilation
    width = w3 // 3
    heads = width // SW_HEAD_DIM
    span = window // dilation
    assert span <= SW_BLOCK and length % SW_BLOCK == 0
    nb = max(1, 8 // dilation)
    rl = nb * SW_BLOCK
    cur = lambda j: pl.BlockSpec((None, dilation, rl, width), lambda b, i: (b, 0, i, j))
    prev = lambda j: pl.BlockSpec((None, dilation, SW_BLOCK, width),
                                  lambda b, i: (b, 0, jnp.maximum(i * nb - 1, 0), j))
    out = pl.BlockSpec((None, width // LANES, rl * dilation, LANES), lambda b, i: (b, 0, i, 0))
    return pl.pallas_call(
        functools.partial(_sw_kernel, dilation=dilation, heads=heads, span=span),
        grid=(bsz, length // rl),
        in_specs=[cur(0), cur(1), cur(2), prev(1), prev(2)],
        out_specs=[out, out],
        out_shape=[jax.ShapeDtypeStruct((bsz, width // LANES, t, LANES), F32)] * 2,
        compiler_params=_params(2),
        name=f"window_attention_d{dilation}",
    )(qkv, qkv, qkv, qkv, qkv)


def _out_kernel(x_ref, ya_ref, yb_ref, o1_ref, l1_ref, o2_ref, l2_ref, o3_ref, l3_ref,
                gate1_ref, shift2_ref, scale2_ref, gate2_ref, fnw_ref,
                wout_ref, wfi_ref, wfo_ref, out_ref, *, ffn_chunks):
    ycs = []
    for j in range(o1_ref.shape[0]):
        l1, l2, l3 = l1_ref[j], l2_ref[j], l3_ref[j]
        m = jnp.maximum(jnp.maximum(l1, l2), l3)
        e1, e2, e3 = jnp.exp(l1 - m), jnp.exp(l2 - m), jnp.exp(l3 - m)
        yc = (e1 * o1_ref[j] + e2 * o2_ref[j] + e3 * o3_ref[j]) / (e1 + e2 + e3)
        ycs.append(yc.astype(BF16))
    y = jnp.concatenate([ya_ref[...].astype(BF16), yb_ref[...].astype(BF16)] + ycs, axis=1)
    x1 = x_ref[...] + gate1_ref[...] * _dot(y, wout_ref[...])
    ms = jnp.mean(x1 * x1, axis=-1, keepdims=True)
    h = x1 * lax.rsqrt(ms + NORM_EPS) * (fnw_ref[...] * (1.0 + scale2_ref[...])) + shift2_ref[...]
    hb = h.astype(BF16)
    hidden = wfo_ref.shape[0]
    fc = hidden // ffn_chunks
    acc = None
    for j in range(ffn_chunks):
        gate = _dot(hb, wfi_ref[:, j * fc:(j + 1) * fc])
        up = _dot(hb, wfi_ref[:, hidden + j * fc:hidden + (j + 1) * fc])
        part = _dot((_silu(gate) * up).astype(BF16), wfo_ref[j * fc:(j + 1) * fc, :])
        acc = part if acc is None else acc + part
    out_ref[...] = x1 + gate2_ref[...] * acc


def _out_ffn(x, ya, yb, attn, mod4, fnw, wout, wfi, wfo, layer, tm):
    bsz, t, d = x.shape
    row = lambda wd: pl.BlockSpec((None, tm, wd), lambda b, i: (b, i, 0))
    modspec = lambda k: pl.BlockSpec((None, None, 1, d), lambda b, i: (b, k, 0, 0))
    attn_flat = [a for pair in attn for a in pair]
    return pl.pallas_call(
        functools.partial(_out_kernel, ffn_chunks=2),
        grid=(bsz, t // tm),
        in_specs=[row(d), row(ya.shape[-1]), row(yb.shape[-1])]
        + [pl.BlockSpec((None, a.shape[1], tm, LANES), lambda b, i: (b, 0, i, 0)) for a in attn_flat]
        + [modspec(2), modspec(3), modspec(4), modspec(5), _resident(fnw.shape),
           _layer_resident(wout, layer), _layer_resident(wfi, layer),
           _layer_resident(wfo, layer)],
        out_specs=row(d),
        out_shape=jax.ShapeDtypeStruct((bsz, t, d), F32),
        compiler_params=_params(2),
        name="out_ffn",
    )(x, ya, yb, *attn_flat, mod4, mod4, mod4, mod4, fnw, wout, wfi, wfo)


def _rotary_tables(t, heads):
    half = ROPE_DIM // 2
    inv = ROPE_THETA ** (-jnp.arange(0, ROPE_DIM, 2, dtype=F32) / ROPE_DIM)
    ang = jnp.arange(t, dtype=F32)[:, None] * inv[None, :]
    cos, sin = jnp.cos(ang), jnp.sin(ang)
    rest = SW_HEAD_DIM - ROPE_DIM
    ones = jnp.ones((t, rest), F32)
    zeros = jnp.zeros((t, rest), F32)
    zh = jnp.zeros((t, half), F32)
    cos_f = jnp.concatenate([cos, cos, ones], axis=1)
    sin_a = jnp.concatenate([-sin, zh, zeros], axis=1)
    sin_b = jnp.concatenate([zh, sin, zeros], axis=1)
    tile = lambda a: jnp.tile(a, (1, heads))
    return tile(cos_f), tile(sin_a), tile(sin_b)


def kernel(x, c, w_mod, b_mod, mix_norm_w, ffn_norm_w, w_in, w_out, dn_conv_w, dn_a_log,
           dn_dt_bias, dn_out_norm_w, gm_ln_g, gm_ln_b, gm_w_s, gm_b_s, sw_q_norm_w,
           sw_k_norm_w, w_ffn_in, w_ffn_out):
    bsz, t, d = x.shape
    nl = w_mod.shape[0]
    dn_heads = dn_a_log.shape[1]
    dn_width = dn_heads * DN_HEAD_DIM
    gm_groups = gm_w_s.shape[1]
    gm_width = gm_groups * GM_GROUP_DIM
    npat = len(SW_PATTERNS)
    sw_width = (w_in.shape[2] - 4 * dn_width - 2 * dn_heads - 2 * gm_width) // (3 * npat)
    sw_heads = sw_width // SW_HEAD_DIM
    assert 2 * dn_heads <= AB_PAD

    o_z = 3 * dn_width
    o_a = o_z + dn_width
    o_gm = o_a + 2 * dn_heads
    o_sw = o_gm + 2 * gm_width
    widths = (3 * dn_width, dn_width, 2 * gm_width) + (3 * sw_width,) * npat + (AB_PAD,)

    mod = _modulation(c, w_mod, b_mod)
    tables = _rotary_tables(t, sw_heads)
    ii = jnp.arange(sw_width) // SW_HEAD_DIM
    ones_bd = ((ii[:, None] == ii[None, :]) * (1.0 / SW_HEAD_DIM)).astype(BF16)
    tm = 512
    ab_cols = jnp.zeros((nl, d, AB_PAD), F32).at[:, :, :2 * dn_heads].set(w_in[:, :, o_a:o_gm])
    w_cat = jnp.concatenate(
        [w_in[:, :, :o_a], w_in[:, :, o_gm:o_sw], w_in[:, :, o_sw:], ab_cols], axis=2).astype(BF16)
    w_out_b, w_ffn_in_b, w_ffn_out_b = (w.astype(BF16) for w in (w_out, w_ffn_in, w_ffn_out))

    for layer in range(nl):
        mod4 = mod[layer].reshape(bsz, 6, 1, d)
        dn_qkv, dn_z, gm_uv, sw1, sw2, sw3, dn_ab = _in_projection(
            x, mod4, mix_norm_w[layer].reshape(1, d), w_cat, layer, tables,
            jnp.tile(sw_q_norm_w[layer] * (SW_HEAD_DIM ** -0.5), sw_heads).reshape(1, sw_width),
            jnp.tile(sw_k_norm_w[layer], sw_heads).reshape(1, sw_width),
            ones_bd, dn_conv_w[layer], widths, tm)

        pad_heads = lambda v: jnp.zeros((1, AB_PAD), F32).at[0, :dn_heads].set(v)
        y_a = _deltanet(dn_qkv, dn_z, dn_ab, pad_heads(dn_a_log[layer]),
                        pad_heads(dn_dt_bias[layer]), dn_out_norm_w[layer].reshape(1, DN_HEAD_DIM),
                        tt=512, nseq=2 if bsz % 2 == 0 else 1)
        bias_full = jnp.repeat(gm_b_s[layer].T, GM_GROUP_DIM, axis=1)
        y_b = _spatial_gating(gm_uv, gm_ln_g[layer].reshape(1, gm_width),
                              gm_ln_b[layer].reshape(1, gm_width), gm_w_s[layer], bias_full,
                              tt=512)
        attn = [_window_attention(sw, window, dilation)
                for sw, (window, dilation) in zip((sw1, sw2, sw3), SW_PATTERNS)]
        x = _out_ffn(x, y_a, y_b, attn, mod4, ffn_norm_w[layer].reshape(1, d),
                     w_out_b, w_ffn_in_b, w_ffn_out_b, layer, tm)
    return x
```

```python
import functools
import math

import jax
import jax.numpy as jnp
from jax import lax
from jax.experimental import pallas as pl
from jax.experimental.pallas import tpu as pltpu

F32 = jnp.float32
BF16 = jnp.bfloat16

NORM_EPS = 1e-6
DN_HEAD_DIM = 128
DN_CONV = 4
DN_CHUNK = 64
DN_INV_BASE = 16
GM_GROUP_DIM = 64
GM_CHUNK = 128
SW_HEAD_DIM = 64
SW_PATTERNS = ((128, 1), (512, 4), (2048, 16))
SW_BLOCK = 128
ROPE_THETA = 500000.0
ROPE_DIM = SW_HEAD_DIM // 4
LANES = 128
AB_PAD = LANES

VMEM_LIMIT_BYTES = 56 * 1024 * 1024


def _params(n_parallel, n_arbitrary=0):
    return pltpu.CompilerParams(
        dimension_semantics=("parallel",) * n_parallel + ("arbitrary",) * n_arbitrary,
        vmem_limit_bytes=VMEM_LIMIT_BYTES,
    )


def _resident(shape):
    zeros = (0,) * len(shape)
    return pl.BlockSpec(shape, lambda *_: zeros, pipeline_mode=pl.Buffered(1))


def _layer_resident(stacked, layer):
    zeros = (0,) * (stacked.ndim - 1)
    return pl.BlockSpec((None,) + stacked.shape[1:], lambda *_: (layer,) + zeros,
                        pipeline_mode=pl.Buffered(1))


def _sigmoid(x):
    return 1.0 / (1.0 + jnp.exp(-x))


def _silu(x):
    return x * _sigmoid(x)


def _dot(a, b):
    return jnp.dot(a, b, preferred_element_type=F32)


def _dot_nt(a, b):
    return lax.dot_general(a, b, (((1,), (1,)), ((), ())), preferred_element_type=F32)


def _dot_tn(a, b):
    return lax.dot_general(a, b, (((0,), (0,)), ((), ())), preferred_element_type=F32)


def _mod_kernel(c_ref, w_ref, b_ref, o_ref):
    c = c_ref[...]
    o_ref[...] = _dot(_silu(c), w_ref[...]) + b_ref[...]


def _modulation(c, w_mod, b_mod):
    nl, d, e = w_mod.shape
    bsz = c.shape[0]
    tn = d
    return pl.pallas_call(
        _mod_kernel,
        grid=(nl, e // tn),
        in_specs=[
            pl.BlockSpec((bsz, d), lambda l, j: (0, 0)),
            pl.BlockSpec((None, d, tn), lambda l, j: (l, 0, j)),
            pl.BlockSpec((None, 1, tn), lambda l, j: (l, 0, j)),
        ],
        out_specs=pl.BlockSpec((None, bsz, tn), lambda l, j: (l, 0, j)),
        out_shape=jax.ShapeDtypeStruct((nl, bsz, e), F32),
        compiler_params=_params(2),
        name="modulation",
    )(c, w_mod, b_mod.reshape(nl, 1, e))


def _group_mean_sq(x, ones_bd, group):
    del group
    return _dot((x * x).astype(BF16), ones_bd)


def _rotary(x, cos_f, sin_a, sin_b):
    n = x.shape[-1]
    half = ROPE_DIM // 2
    return x * cos_f + pltpu.roll(x, n - half, 1) * sin_a + pltpu.roll(x, half, 1) * sin_b


def _in_kernel(x_ref, shift_ref, scale_ref, nw_ref, w_ref, cos_ref, sina_ref, sinb_ref,
               qnw_ref, knw_ref, bd_ref, cw_ref,
               dn_ref, z_ref, gm_ref, sw1_ref, sw2_ref, sw3_ref, ab_ref, stage_ref, hist_ref,
               *, widths):
    tm = x_ref.shape[0]
    x = x_ref[...]
    ms = jnp.mean(x * x, axis=-1, keepdims=True)
    h = x * lax.rsqrt(ms + NORM_EPS) * (nw_ref[...] * (1.0 + scale_ref[...])) + shift_ref[...]
    hb = h.astype(BF16)
    offs = [0]
    for wd in widths:
        offs.append(offs[-1] + wd)
    plain = ((z_ref, 1), (gm_ref, 2), (ab_ref, 6))
    for ref, idx in plain:
        ref[...] = _dot(hb, w_ref[:, offs[idx]:offs[idx + 1]])

    pad = hist_ref.shape[1] - tm
    nblk = hist_ref.shape[0]

    @pl.when(pl.program_id(1) == 0)
    def _():
        hist_ref[:, 0:pad, :] = jnp.zeros((nblk, pad, LANES), F32)

    @pl.when(pl.program_id(1) > 0)
    def _():
        hist_ref[:, 0:pad, :] = hist_ref[:, tm:tm + pad, :]

    raw = _dot(hb, w_ref[:, offs[0]:offs[1]])
    for cb in range(nblk):
        hist_ref[cb, pad:pad + tm, :] = raw[:, cb * LANES:(cb + 1) * LANES]
    phases = 8
    per = tm // phases
    for cb in range(nblk):
        cols = slice(cb * LANES, (cb + 1) * LANES)
        taps = [cw_ref[tap:tap + 1, cols] for tap in range(DN_CONV)]
        x = {j: hist_ref[cb, pl.ds(pad + j, per, stride=phases), :]
             for j in range(1 - DN_CONV, phases)}
        for s in range(phases):
            y = taps[DN_CONV - 1] * x[s]
            for tap in range(DN_CONV - 1):
                y = y + taps[tap] * x[s - (DN_CONV - 1 - tap)]
            y = _silu(y)
            if cb < 2 * nblk // 3:
                y = y * lax.rsqrt(jnp.sum(y * y, axis=-1, keepdims=True) + NORM_EPS)
            if cb < nblk // 3:
                y = y * (DN_HEAD_DIM ** -0.5)
            dn_ref[cb, pl.ds(s, per, stride=phases), :] = y
    sww = widths[3] // 3
    cos_f, sin_a, sin_b = cos_ref[...], sina_ref[...], sinb_ref[...]
    bd = bd_ref[...]
    for ref, idx in ((sw1_ref, 3), (sw2_ref, 4), (sw3_ref, 5)):
        raw = _dot(hb, w_ref[:, offs[idx]:offs[idx + 1]])
        q, k, v = raw[:, :sww], raw[:, sww:2 * sww], raw[:, 2 * sww:]
        qn = q * lax.rsqrt(_group_mean_sq(q, bd, SW_HEAD_DIM) + NORM_EPS) * qnw_ref[...]
        kn = k * lax.rsqrt(_group_mean_sq(k, bd, SW_HEAD_DIM) + NORM_EPS) * knw_ref[...]
        qn = _rotary(qn, cos_f, sin_a, sin_b)
        kn = _rotary(kn, cos_f, sin_a, sin_b)
        dilation = ref.shape[0]
        if dilation == 1:
            ref[0, :, :sww] = qn.astype(BF16)
            ref[0, :, sww:2 * sww] = kn.astype(BF16)
            ref[0, :, 2 * sww:] = v.astype(BF16)
        else:
            parts = (qn, kn, v)
            per = sww // LANES
            for j in range(3 * per):
                stage_ref[j] = parts[j // per][:, (j % per) * LANES:(j % per + 1) * LANES]
            for r in range(dilation):
                for j in range(3 * per):
                    ref[r, :, j * LANES:(j + 1) * LANES] = stage_ref[
                        j, pl.ds(r, tm // dilation, stride=dilation), :].astype(BF16)


def _in_projection(x, mod4, nw, w_cat, layer, tables, qnw, knw, ones_bd, conv_w, widths, tm):
    bsz, t, d = x.shape
    cos_f, sin_a, sin_b = tables
    sww = widths[3] // 3
    row = lambda wd: pl.BlockSpec((None, tm, wd), lambda b, i: (b, i, 0))
    modspec = lambda k: pl.BlockSpec((None, None, 1, d), lambda b, i: (b, k, 0, 0))
    tabspec = pl.BlockSpec((tm, sww), lambda b, i: (i, 0))
    sw_specs = [pl.BlockSpec((None, dil, tm // dil, 3 * sww), lambda b, i: (b, 0, i, 0))
                for _, dil in SW_PATTERNS]
    sw_shapes = [jax.ShapeDtypeStruct((bsz, dil, t // dil, 3 * sww), BF16) for _, dil in SW_PATTERNS]
    f32_out = lambda wd: jax.ShapeDtypeStruct((bsz, t, wd), F32)
    return pl.pallas_call(
        functools.partial(_in_kernel, widths=widths),
        grid=(bsz, t // tm),
        in_specs=[
            row(d), modspec(0), modspec(1), _resident(nw.shape), _layer_resident(w_cat, layer),
            tabspec, tabspec, tabspec, _resident(qnw.shape), _resident(knw.shape),
            _resident(ones_bd.shape), _resident(conv_w.shape),
        ],
        out_specs=[pl.BlockSpec((None, widths[0] // LANES, tm, LANES), lambda b, i: (b, 0, i, 0)),
                   row(widths[1]), row(widths[2])] + sw_specs + [row(widths[6])],
        out_shape=[jax.ShapeDtypeStruct((bsz, widths[0] // LANES, t, LANES), F32),
                   f32_out(widths[1]), f32_out(widths[2])] + sw_shapes
        + [f32_out(widths[6])],
        scratch_shapes=[pltpu.VMEM((3 * sww // LANES, tm, LANES), F32),
                        pltpu.VMEM((widths[0] // LANES, tm + 8, LANES), F32)],
        compiler_params=_params(1, 1),
        name="in_projection",
    )(x, mod4, mod4, nw, w_cat, cos_f, sin_a, sin_b, qnw, knw, ones_bd, conv_w)


def _expand(x, e):
    x1 = x.astype(BF16)
    r = x - x1.astype(F32)
    x2 = r.astype(BF16)
    x3 = (r - x2.astype(F32)).astype(BF16)
    return _dot(x1, e) + _dot(x2, e) + _dot(x3, e)


def _block_diag(x, blk):
    c, n = x.shape
    zero = jnp.zeros((c, LANES), x.dtype)
    low = lax.broadcasted_iota(jnp.int32, (c, LANES), 1) < blk
    out = []
    for h in range(n // blk):
        g = (h * blk) // LANES
        piece = x[:, g * LANES:(g + 1) * LANES]
        if blk < LANES:
            piece = jnp.where(low if (h * blk) % LANES == 0 else jnp.logical_not(low), piece, zero)
        out.append(jnp.concatenate([piece if j == g else zero for j in range(n // LANES)], axis=1))
    return jnp.concatenate(out, axis=0)


def _dn_kernel(qkv_ref, z_ref, ab_ref, alog_ref, dtb_ref, onw_ref, eg64_ref, eg128_ref,
               eb128_ref, o_ref, q_s, qd_s, k_s, kb_s, kd_s, vb_s, wr_s, eg_s, gcat_s, s_ref,
               *, tt, chunk, heads):
    hd = DN_HEAD_DIM
    nseq = qkv_ref.shape[0]
    width = heads * hd
    cat = heads * chunk
    nchunks = tt // chunk
    assert hd == LANES and 2 * chunk == LANES

    @pl.when(pl.program_id(1) == 0)
    def _():
        s_ref[...] = jnp.zeros_like(s_ref)

    for b in range(nseq):
        ab = ab_ref[b]
        sp = ab + dtb_ref[...]
        softplus = jnp.maximum(sp, 0.0) + jnp.log1p(jnp.exp(-jnp.abs(sp)))
        g = -jnp.exp(alog_ref[...]) * softplus
        pos = lax.broadcasted_iota(jnp.int32, g.shape, 0) % chunk
        step = 1
        while step < chunk:
            g = g + jnp.where(pos >= step, pltpu.roll(g, step, 0), 0.0)
            step *= 2
        gcat_s[b] = _expand(g, eg64_ref[...])
        gexp = _expand(g, eg128_ref[...])
        bexp = _expand(_sigmoid(ab), eb128_ref[...])
        eg = jnp.exp(gexp)
        eg_s[b] = eg
        g_last = jnp.concatenate(
            [jnp.broadcast_to(gexp[(c + 1) * chunk - 1:(c + 1) * chunk, :], (chunk, width))
             for c in range(nchunks)], axis=0)
        kd_scale = jnp.exp(g_last - gexp)

        for h in range(heads):
            hc = slice(h * hd, (h + 1) * hd)
            q, k, v = qkv_ref[b, h], qkv_ref[b, heads + h], qkv_ref[b, 2 * heads + h]
            kb = k * bexp[:, hc]
            q_s[b, :, hc] = q.astype(BF16)
            qd_s[b, :, hc] = (q * eg[:, hc]).astype(BF16)
            k_s[b, :, hc] = k.astype(BF16)
            kb_s[b, :, hc] = kb.astype(BF16)
            kd_s[b, :, hc] = (k * kd_scale[:, hc]).astype(BF16)
            vb_s[b, :, hc] = (v * bexp[:, hc]).astype(BF16)
            wr_s[b, :, hc] = (kb * eg[:, hc]).astype(BF16)

    ri = lax.broadcasted_iota(jnp.int32, (chunk, cat), 0)
    li = lax.broadcasted_iota(jnp.int32, (chunk, cat), 1) % chunk
    causal = ri >= li
    strict = ri > li
    diag = ri == li
    eye_cat = jnp.where(diag, 1.0, 0.0).astype(F32)
    rows = [slice(c * chunk, (c + 1) * chunk) for c in range(nchunks)]
    units = [(b, c) for c in range(nchunks) for b in range(nseq)]
    nunits = len(units)

    a_all, a_qk = [], []
    for b, c in units:
        k_bd = _block_diag(k_s[b, rows[c], :], hd)
        both = _dot_nt(jnp.concatenate([kb_s[b, rows[c], :], q_s[b, rows[c], :]], axis=0), k_bd)
        gc = gcat_s[b, rows[c], :]
        g_row = jnp.sum(jnp.where(diag, gc, 0.0), axis=0, keepdims=True)
        decay = jnp.where(causal, jnp.exp(gc - g_row), 0.0)
        a_low = jnp.where(strict, both[:chunk] * decay, 0.0)
        a_qk.append((both[chunk:] * decay).astype(BF16))
        a_all.append(a_low)
    same = lambda s: (ri // s) == (li // s)
    in_base = same(DN_INV_BASE)
    inv, pw = [], []
    for i in range(nunits):
        d_f = jnp.where(in_base, a_all[i], 0.0)
        d_b = d_f.astype(BF16)
        inv.append(eye_cat - d_f)
        pw.append(_dot(d_b, _block_diag(d_b, chunk)))
    n_base = int(math.log2(DN_INV_BASE)) - 1
    for it in range(n_base):
        for i in range(nunits):
            pw_b = pw[i].astype(BF16)
            pw_bd = _block_diag(pw_b, chunk)
            if it + 1 < n_base:
                r = _dot(jnp.concatenate([inv[i].astype(BF16), pw_b], axis=0), pw_bd)
                inv[i] = inv[i] + r[:chunk]
                pw[i] = r[chunk:]
            else:
                inv[i] = inv[i] + _dot(inv[i].astype(BF16), pw_bd)
    size = DN_INV_BASE
    while size < chunk:
        off_diag = same(2 * size) & jnp.logical_not(same(size))
        for i in range(nunits):
            t_b = inv[i].astype(BF16)
            n_b = jnp.where(off_diag, a_all[i], 0.0).astype(BF16)
            t_n = _dot(t_b, _block_diag(n_b, chunk))
            inv[i] = inv[i] - _dot(t_n.astype(BF16), _block_diag(t_b, chunk))
        size *= 2
    hcols = [slice(h * hd, (h + 1) * hd) for h in range(heads)]
    hrows = [slice(h * chunk, (h + 1) * chunk) for h in range(heads)]
    stack = lambda ref, b, c: jnp.concatenate(
        [ref[b, rows[c], hcols[h]] for h in range(heads)], axis=0)
    u, w = [], []
    for i, (b, c) in enumerate(units):
        rhs = jnp.concatenate([stack(vb_s, b, c), stack(wr_s, b, c)], axis=1)
        uw = _dot(_block_diag(inv[i].astype(BF16), chunk), rhs)
        u.append(uw[:, :hd])
        w.append(uw[:, hd:].astype(BF16))

    state = [[s_ref[b, h] for h in range(heads)] for b in range(nseq)]
    for i, (b, c) in enumerate(units):
        res = [_dot(jnp.concatenate([w[i][hrows[h], :], qd_s[b, rows[c], hcols[h]]], axis=0),
                    state[b][h].astype(BF16)) for h in range(heads)]
        v_new = u[i] - jnp.concatenate([r[:chunk] for r in res], axis=0)
        v_new_b = v_new.astype(BF16)
        o = (jnp.concatenate([r[chunk:] for r in res], axis=0)
             + _dot(_block_diag(a_qk[i], chunk), v_new_b))
        last = (c + 1) * chunk - 1
        for h in range(heads):
            o_ref[b, rows[c], hcols[h]] = o[hrows[h], :]
            upd = _dot_tn(kd_s[b, rows[c], hcols[h]], v_new_b[hrows[h], :])
            state[b][h] = state[b][h] * eg_s[b, last:last + 1, hcols[h]] + upd
    for b in range(nseq):
        for h in range(heads):
            s_ref[b, h] = state[b][h]

    for b in range(nseq):
        for h in range(heads):
            o = o_ref[b, :, hcols[h]]
            on = o * lax.rsqrt(jnp.mean(o * o, axis=-1, keepdims=True) + NORM_EPS) * onw_ref[...]
            o_ref[b, :, hcols[h]] = on * _silu(z_ref[b, :, hcols[h]])


def _deltanet(qkv, z, ab, alog_pad, dtb_pad, onw, tt, nseq):
    bsz, nblk, t, _ = qkv.shape
    heads = nblk // 3
    width = heads * DN_HEAD_DIM
    chunk = DN_CHUNK
    cat = heads * chunk
    hh = jnp.arange(AB_PAD)[:, None]
    e_g64 = (hh == jnp.arange(cat)[None, :] // chunk).astype(BF16)
    e_g128 = (hh == jnp.arange(width)[None, :] // DN_HEAD_DIM).astype(BF16)
    e_b128 = (hh == heads + jnp.arange(width)[None, :] // DN_HEAD_DIM).astype(BF16)
    row = lambda wd: pl.BlockSpec((nseq, tt, wd), lambda b, i: (b, i, 0))
    return pl.pallas_call(
        functools.partial(_dn_kernel, tt=tt, chunk=chunk, heads=heads),
        grid=(bsz // nseq, t // tt),
        in_specs=[pl.BlockSpec((nseq, nblk, tt, LANES), lambda b, i: (b, 0, i, 0)),
                  row(width), row(AB_PAD),
                  _resident(alog_pad.shape), _resident(dtb_pad.shape), _resident(onw.shape),
                  _resident(e_g64.shape), _resident(e_g128.shape), _resident(e_b128.shape)],
        out_specs=row(width),
        out_shape=jax.ShapeDtypeStruct((bsz, t, width), F32),
        scratch_shapes=[pltpu.VMEM((nseq, tt, width), BF16)] * 7
        + [pltpu.VMEM((nseq, tt, width), F32), pltpu.VMEM((nseq, tt, cat), F32),
           pltpu.VMEM((nseq, heads, DN_HEAD_DIM, DN_HEAD_DIM), F32)],
        compiler_params=_params(1, 1),
        name="deltanet",
    )(qkv, z, ab, alog_pad, dtb_pad, onw, e_g64, e_g128, e_b128)


def _gm_kernel(uv_ref, lng_ref, lnb_ref, ws_ref, bias_ref, o_ref, *, tt, groups):
    width = groups * GM_GROUP_DIM
    x = uv_ref[...]
    zz = 0.5 * x * (1.0 + lax.erf(x * (2.0 ** -0.5)))
    u, v = zz[:, :width], zz[:, width:]
    mu = jnp.mean(v, axis=-1, keepdims=True)
    vc = v - mu
    var = jnp.mean(vc * vc, axis=-1, keepdims=True)
    vn = (vc * lax.rsqrt(var + NORM_EPS) * lng_ref[...] + lnb_ref[...]).astype(BF16)
    ii = lax.broadcasted_iota(jnp.int32, (GM_CHUNK, GM_CHUNK), 0)
    jj = lax.broadcasted_iota(jnp.int32, (GM_CHUNK, GM_CHUNK), 1)
    ws = [jnp.where(ii >= jj, ws_ref[g], 0.0).astype(BF16) for g in range(groups)]
    bias = bias_ref[...]
    for c in range(tt // GM_CHUNK):
        rows = slice(c * GM_CHUNK, (c + 1) * GM_CHUNK)
        parts = [_dot(ws[g], vn[rows, g * GM_GROUP_DIM:(g + 1) * GM_GROUP_DIM])
                 for g in range(groups)]
        o_ref[rows, :] = u[rows, :] * (jnp.concatenate(parts, axis=1) + bias)


def _spatial_gating(uv, ln_g, ln_b, w_s, bias_full, tt):
    bsz, t, w2 = uv.shape
    width = w2 // 2
    groups = width // GM_GROUP_DIM
    return pl.pallas_call(
        functools.partial(_gm_kernel, tt=tt, groups=groups),
        grid=(bsz, t // tt),
        in_specs=[pl.BlockSpec((None, tt, w2), lambda b, i: (b, i, 0)),
                  _resident(ln_g.shape), _resident(ln_b.shape), _resident(w_s.shape),
                  _resident(bias_full.shape)],
        out_specs=pl.BlockSpec((None, tt, width), lambda b, i: (b, i, 0)),
        out_shape=jax.ShapeDtypeStruct((bsz, t, width), F32),
        compiler_params=_params(2),
        name="spatial_gating",
    )(uv, ln_g, ln_b, w_s, bias_full)


def _sw_kernel(q_ref, k_ref, v_ref, kp_ref, vp_ref, o_ref, lse_ref, *, dilation, heads, span):
    blk = SW_BLOCK
    hd = SW_HEAD_DIM
    nk = 2 * blk
    nqb = q_ref.shape[1] // blk
    assert 2 * hd == LANES and heads % 2 == 0
    a = lax.broadcasted_iota(jnp.int32, (2 * blk, nk), 0) % blk
    c = lax.broadcasted_iota(jnp.int32, (2 * blk, nk), 1)
    dist = blk + a - c
    in_window = (dist >= 0) & (dist <= span)
    own_block = c >= blk
    low = lax.broadcasted_iota(jnp.int32, (blk, LANES), 1) < hd
    ones = jnp.ones((nk, LANES), BF16)
    units = [(r, qb, j) for r in range(dilation) for qb in range(nqb) for j in range(heads // 2)]
    group = 4
    for g0 in range(0, len(units), group):
        grp = units[g0:g0 + group]
        ss, vos = [], []
        for r, qb, j in grp:
            rows = slice(qb * blk, (qb + 1) * blk)
            lanes = slice(j * LANES, (j + 1) * LANES)
            if qb == 0:
                kk = jnp.concatenate([kp_ref[r, :, lanes], k_ref[r, rows, lanes]], axis=0)
                vv = jnp.concatenate([vp_ref[r, :, lanes], v_ref[r, rows, lanes]], axis=0)
            else:
                kk = k_ref[r, (qb - 1) * blk:(qb + 1) * blk, lanes]
                vv = v_ref[r, (qb - 1) * blk:(qb + 1) * blk, lanes]
            q = q_ref[r, rows, lanes]
            zero = jnp.zeros_like(q)
            q2 = jnp.concatenate([jnp.where(low, q, zero), jnp.where(low, zero, q)], axis=0)
            s = _dot_nt(q2, kk)
            has_prev = (pl.program_id(1) * nqb + qb) > 0
            ss.append(jnp.where(in_window & (own_block | has_prev), s, -jnp.inf))
            vos.append(jnp.concatenate([vv, ones], axis=1))
        ms, ps = [], []
        for s in ss:
            m = jnp.max(s, axis=-1, keepdims=True)
            ps.append(jnp.exp(s - m).astype(BF16))
            ms.append(jnp.where(low, jnp.broadcast_to(m[:blk], (blk, LANES)),
                                jnp.broadcast_to(m[blk:], (blk, LANES))))
        for (r, qb, j), p, vo, m in zip(grp, ps, vos, ms):
            pv = _dot(p, vo)
            o = jnp.where(low, pv[:blk, :LANES], pv[blk:, :LANES])
            l = jnp.where(low, pv[:blk, LANES:], pv[blk:, LANES:])
            if dilation == 1:
                tok = slice(qb * blk, (qb + 1) * blk)
            else:
                tok = pl.ds(r + dilation * qb * blk, blk, stride=dilation)
            o_ref[j, tok, :] = o / l
            lse_ref[j, tok, :] = m + jnp.log(l)


def _window_attention(qkv, window, dilation):
    bsz, _, length, w3 = qkv.shape
    t = length * dilation
    width = w3 // 3
    heads = width // SW_HEAD_DIM
    span = window // dilation
    assert span <= SW_BLOCK and length % SW_BLOCK == 0
    nb = max(1, 8 // dilation)
    rl = nb * SW_BLOCK
    cur = lambda j: pl.BlockSpec((None, dilation, rl, width), lambda b, i: (b, 0, i, j))
    prev = lambda j: pl.BlockSpec((None, dilation, SW_BLOCK, width),
                                  lambda b, i: (b, 0, jnp.maximum(i * nb - 1, 0), j))
    out = pl.BlockSpec((None, width // LANES, rl * dilation, LANES), lambda b, i: (b, 0, i, 0))
    return pl.pallas_call(
        functools.partial(_sw_kernel, dilation=dilation, heads=heads, span=span),
        grid=(bsz, length // rl),
        in_specs=[cur(0), cur(1), cur(2), prev(1), prev(2)],
        out_specs=[out, out],
        out_shape=[jax.ShapeDtypeStruct((bsz, width // LANES, t, LANES), F32)] * 2,
        compiler_params=_params(2),
        name=f"window_attention_d{dilation}",
    )(qkv, qkv, qkv, qkv, qkv)


def _out_kernel(x_ref, ya_ref, yb_ref, o1_ref, l1_ref, o2_ref, l2_ref, o3_ref, l3_ref,
                gate1_ref, shift2_ref, scale2_ref, gate2_ref, fnw_ref,
                wout_ref, wfi_ref, wfo_ref, out_ref, *, ffn_chunks):
    ycs = []
    for j in range(o1_ref.shape[0]):
        l1, l2, l3 = l1_ref[j], l2_ref[j], l3_ref[j]
        m = jnp.maximum(jnp.maximum(l1, l2), l3)
        e1, e2, e3 = jnp.exp(l1 - m), jnp.exp(l2 - m), jnp.exp(l3 - m)
        yc = (e1 * o1_ref[j] + e2 * o2_ref[j] + e3 * o3_ref[j]) / (e1 + e2 + e3)
        ycs.append(yc.astype(BF16))
    y = jnp.concatenate([ya_ref[...].astype(BF16), yb_ref[...].astype(BF16)] + ycs, axis=1)
    x1 = x_ref[...] + gate1_ref[...] * _dot(y, wout_ref[...])
    ms = jnp.mean(x1 * x1, axis=-1, keepdims=True)
    h = x1 * lax.rsqrt(ms + NORM_EPS) * (fnw_ref[...] * (1.0 + scale2_ref[...])) + shift2_ref[...]
    hb = h.astype(BF16)
    hidden = wfo_ref.shape[0]
    fc = hidden // ffn_chunks
    acc = None
    for j in range(ffn_chunks):
        gate = _dot(hb, wfi_ref[:, j * fc:(j + 1) * fc])
        up = _dot(hb, wfi_ref[:, hidden + j * fc:hidden + (j + 1) * fc])
        part = _dot((_silu(gate) * up).astype(BF16), wfo_ref[j * fc:(j + 1) * fc, :])
        acc = part if acc is None else acc + part
    out_ref[...] = x1 + gate2_ref[...] * acc


def _out_ffn(x, ya, yb, attn, mod4, fnw, wout, wfi, wfo, layer, tm):
    bsz, t, d = x.shape
    row = lambda wd: pl.BlockSpec((None, tm, wd), lambda b, i: (b, i, 0))
    modspec = lambda k: pl.BlockSpec((None, None, 1, d), lambda b, i: (b, k, 0, 0))
    attn_flat = [a for pair in attn for a in pair]
    return pl.pallas_call(
        functools.partial(_out_kernel, ffn_chunks=1),
        grid=(bsz, t // tm),
        in_specs=[row(d), row(ya.shape[-1]), row(yb.shape[-1])]
        + [pl.BlockSpec((None, a.shape[1], tm, LANES), lambda b, i: (b, 0, i, 0)) for a in attn_flat]
        + [modspec(2), modspec(3), modspec(4), modspec(5), _resident(fnw.shape),
           _layer_resident(wout, layer), _layer_resident(wfi, layer),
           _layer_resident(wfo, layer)],
        out_specs=row(d),
        out_shape=jax.ShapeDtypeStruct((bsz, t, d), F32),
        compiler_params=_params(2),
        name="out_ffn",
    )(x, ya, yb, *attn_flat, mod4, mod4, mod4, mod4, fnw, wout, wfi, wfo)


def _rotary_tables(t, heads):
    half = ROPE_DIM // 2
    inv = ROPE_THETA ** (-jnp.arange(0, ROPE_DIM, 2, dtype=F32) / ROPE_DIM)
    ang = jnp.arange(t, dtype=F32)[:, None] * inv[None, :]
    cos, sin = jnp.cos(ang), jnp.sin(ang)
    rest = SW_HEAD_DIM - ROPE_DIM
    ones = jnp.ones((t, rest), F32)
    zeros = jnp.zeros((t, rest), F32)
    zh = jnp.zeros((t, half), F32)
    cos_f = jnp.concatenate([cos, cos, ones], axis=1)
    sin_a = jnp.concatenate([-sin, zh, zeros], axis=1)
    sin_b = jnp.concatenate([zh, sin, zeros], axis=1)
    tile = lambda a: jnp.tile(a, (1, heads))
    return tile(cos_f), tile(sin_a), tile(sin_b)


def kernel(x, c, w_mod, b_mod, mix_norm_w, ffn_norm_w, w_in, w_out, dn_conv_w, dn_a_log,
           dn_dt_bias, dn_out_norm_w, gm_ln_g, gm_ln_b, gm_w_s, gm_b_s, sw_q_norm_w,
           sw_k_norm_w, w_ffn_in, w_ffn_out):
    bsz, t, d = x.shape
    nl = w_mod.shape[0]
    dn_heads = dn_a_log.shape[1]
    dn_width = dn_heads * DN_HEAD_DIM
    gm_groups = gm_w_s.shape[1]
    gm_width = gm_groups * GM_GROUP_DIM
    npat = len(SW_PATTERNS)
    sw_width = (w_in.shape[2] - 4 * dn_width - 2 * dn_heads - 2 * gm_width) // (3 * npat)
    sw_heads = sw_width // SW_HEAD_DIM
    assert 2 * dn_heads <= AB_PAD

    o_z = 3 * dn_width
    o_a = o_z + dn_width
    o_gm = o_a + 2 * dn_heads
    o_sw = o_gm + 2 * gm_width
    widths = (3 * dn_width, dn_width, 2 * gm_width) + (3 * sw_width,) * npat + (AB_PAD,)

    mod = _modulation(c, w_mod, b_mod)
    tables = _rotary_tables(t, sw_heads)
    ii = jnp.arange(sw_width) // SW_HEAD_DIM
    ones_bd = ((ii[:, None] == ii[None, :]) * (1.0 / SW_HEAD_DIM)).astype(BF16)
    tm = 512
    ab_cols = jnp.zeros((nl, d, AB_PAD), F32).at[:, :, :2 * dn_heads].set(w_in[:, :, o_a:o_gm])
    w_cat = jnp.concatenate(
        [w_in[:, :, :o_a], w_in[:, :, o_gm:o_sw], w_in[:, :, o_sw:], ab_cols], axis=2).astype(BF16)
    w_out_b, w_ffn_in_b, w_ffn_out_b = (w.astype(BF16) for w in (w_out, w_ffn_in, w_ffn_out))

    for layer in range(nl):
        mod4 = mod[layer].reshape(bsz, 6, 1, d)
        dn_qkv, dn_z, gm_uv, sw1, sw2, sw3, dn_ab = _in_projection(
            x, mod4, mix_norm_w[layer].reshape(1, d), w_cat, layer, tables,
            jnp.tile(sw_q_norm_w[layer] * (SW_HEAD_DIM ** -0.5), sw_heads).reshape(1, sw_width),
            jnp.tile(sw_k_norm_w[layer], sw_heads).reshape(1, sw_width),
            ones_bd, dn_conv_w[layer], widths, tm)

        pad_heads = lambda v: jnp.zeros((1, AB_PAD), F32).at[0, :dn_heads].set(v)
        y_a = _deltanet(dn_qkv, dn_z, dn_ab, pad_heads(dn_a_log[layer]),
                        pad_heads(dn_dt_bias[layer]), dn_out_norm_w[layer].reshape(1, DN_HEAD_DIM),
                        tt=512, nseq=2 if bsz % 2 == 0 else 1)
        bias_full = jnp.repeat(gm_b_s[layer].T, GM_GROUP_DIM, axis=1)
        y_b = _spatial_gating(gm_uv, gm_ln_g[layer].reshape(1, gm_width),
                              gm_ln_b[layer].reshape(1, gm_width), gm_w_s[layer], bias_full,
                              tt=1024)
        attn = [_window_attention(sw, window, dilation)
                for sw, (window, dilation) in zip((sw1, sw2, sw3), SW_PATTERNS)]
        x = _out_ffn(x, y_a, y_b, attn, mod4, ffn_norm_w[layer].reshape(1, d),
                     w_out_b, w_ffn_in_b, w_ffn_out_b, layer, tm)
    return x
```

```python
import functools
import math

import jax
import jax.numpy as jnp
from jax import lax
from jax.experimental import pallas as pl
from jax.experimental.pallas import tpu as pltpu

F32 = jnp.float32
BF16 = jnp.bfloat16

NORM_EPS = 1e-6
DN_HEAD_DIM = 128
DN_CONV = 4
DN_CHUNK = 64
DN_INV_BASE = 16
GM_GROUP_DIM = 64
GM_CHUNK = 128
SW_HEAD_DIM = 64
SW_PATTERNS = ((128, 1), (512, 4), (2048, 16))
SW_BLOCK = 128
ROPE_THETA = 500000.0
ROPE_DIM = SW_HEAD_DIM // 4
LANES = 128
AB_PAD = LANES

VMEM_LIMIT_BYTES = 56 * 1024 * 1024


def _params(n_parallel, n_arbitrary=0):
    return pltpu.CompilerParams(
        dimension_semantics=("parallel",) * n_parallel + ("arbitrary",) * n_arbitrary,
        vmem_limit_bytes=VMEM_LIMIT_BYTES,
    )


def _resident(shape):
    zeros = (0,) * len(shape)
    return pl.BlockSpec(shape, lambda *_: zeros, pipeline_mode=pl.Buffered(1))


def _layer_resident(stacked, layer):
    zeros = (0,) * (stacked.ndim - 1)
    return pl.BlockSpec((None,) + stacked.shape[1:], lambda *_: (layer,) + zeros,
                        pipeline_mode=pl.Buffered(1))


def _sigmoid(x):
    return 1.0 / (1.0 + jnp.exp(-x))


def _silu(x):
    return x * _sigmoid(x)


def _dot(a, b):
    return jnp.dot(a, b, preferred_element_type=F32)


def _dot_nt(a, b):
    return lax.dot_general(a, b, (((1,), (1,)), ((), ())), preferred_element_type=F32)


def _dot_tn(a, b):
    return lax.dot_general(a, b, (((0,), (0,)), ((), ())), preferred_element_type=F32)


def _mod_kernel(c_ref, w_ref, b_ref, o_ref):
    c = c_ref[...]
    o_ref[...] = _dot(_silu(c), w_ref[...]) + b_ref[...]


def _modulation(c, w_mod, b_mod):
    nl, d, e = w_mod.shape
    bsz = c.shape[0]
    tn = d
    return pl.pallas_call(
        _mod_kernel,
        grid=(nl, e // tn),
        in_specs=[
            pl.BlockSpec((bsz, d), lambda l, j: (0, 0)),
            pl.BlockSpec((None, d, tn), lambda l, j: (l, 0, j)),
            pl.BlockSpec((None, 1, tn), lambda l, j: (l, 0, j)),
        ],
        out_specs=pl.BlockSpec((None, bsz, tn), lambda l, j: (l, 0, j)),
        out_shape=jax.ShapeDtypeStruct((nl, bsz, e), F32),
        compiler_params=_params(2),
        name="modulation",
    )(c, w_mod, b_mod.reshape(nl, 1, e))


def _group_mean_sq(x, ones_bd, group):
    del group
    return _dot((x * x).astype(BF16), ones_bd)


def _rotary(x, cos_f, sin_a, sin_b):
    n = x.shape[-1]
    half = ROPE_DIM // 2
    return x * cos_f + pltpu.roll(x, n - half, 1) * sin_a + pltpu.roll(x, half, 1) * sin_b


def _in_kernel(x_ref, shift_ref, scale_ref, nw_ref, w_ref, cos_ref, sina_ref, sinb_ref,
               qnw_ref, knw_ref, bd_ref, cw_ref,
               dn_ref, z_ref, gm_ref, sw1_ref, sw2_ref, sw3_ref, ab_ref, stage_ref, hist_ref,
               *, widths):
    tm = x_ref.shape[0]
    x = x_ref[...]
    ms = jnp.mean(x * x, axis=-1, keepdims=True)
    h = x * lax.rsqrt(ms + NORM_EPS) * (nw_ref[...] * (1.0 + scale_ref[...])) + shift_ref[...]
    hb = h.astype(BF16)
    offs = [0]
    for wd in widths:
        offs.append(offs[-1] + wd)
    plain = ((z_ref, 1), (gm_ref, 2), (ab_ref, 6))
    for ref, idx in plain:
        ref[...] = _dot(hb, w_ref[:, offs[idx]:offs[idx + 1]])

    pad = hist_ref.shape[1] - tm
    nblk = hist_ref.shape[0]

    @pl.when(pl.program_id(1) == 0)
    def _():
        hist_ref[:, 0:pad, :] = jnp.zeros((nblk, pad, LANES), F32)

    @pl.when(pl.program_id(1) > 0)
    def _():
        hist_ref[:, 0:pad, :] = hist_ref[:, tm:tm + pad, :]

    raw = _dot(hb, w_ref[:, offs[0]:offs[1]])
    for cb in range(nblk):
        hist_ref[cb, pad:pad + tm, :] = raw[:, cb * LANES:(cb + 1) * LANES]
    phases = 8
    per = tm // phases
    for cb in range(nblk):
        cols = slice(cb * LANES, (cb + 1) * LANES)
        taps = [cw_ref[tap:tap + 1, cols] for tap in range(DN_CONV)]
        x = {j: hist_ref[cb, pl.ds(pad + j, per, stride=phases), :]
             for j in range(1 - DN_CONV, phases)}
        for s in range(phases):
            y = taps[DN_CONV - 1] * x[s]
            for tap in range(DN_CONV - 1):
                y = y + taps[tap] * x[s - (DN_CONV - 1 - tap)]
            y = _silu(y)
            if cb < 2 * nblk // 3:
                y = y * lax.rsqrt(jnp.sum(y * y, axis=-1, keepdims=True) + NORM_EPS)
            if cb < nblk // 3:
                y = y * (DN_HEAD_DIM ** -0.5)
            dn_ref[cb, pl.ds(s, per, stride=phases), :] = y
    sww = widths[3] // 3
    cos_f, sin_a, sin_b = cos_ref[...], sina_ref[...], sinb_ref[...]
    bd = bd_ref[...]
    for ref, idx in ((sw1_ref, 3), (sw2_ref, 4), (sw3_ref, 5)):
        raw = _dot(hb, w_ref[:, offs[idx]:offs[idx + 1]])
        q, k, v = raw[:, :sww], raw[:, sww:2 * sww], raw[:, 2 * sww:]
        qn = q * lax.rsqrt(_group_mean_sq(q, bd, SW_HEAD_DIM) + NORM_EPS) * qnw_ref[...]
        kn = k * lax.rsqrt(_group_mean_sq(k, bd, SW_HEAD_DIM) + NORM_EPS) * knw_ref[...]
        qn = _rotary(qn, cos_f, sin_a, sin_b)
        kn = _rotary(kn, cos_f, sin_a, sin_b)
        dilation = ref.shape[0]
        if dilation == 1:
            ref[0, :, :sww] = qn.astype(BF16)
            ref[0, :, sww:2 * sww] = kn.astype(BF16)
            ref[0, :, 2 * sww:] = v.astype(BF16)
        else:
            parts = (qn, kn, v)
            per = sww // LANES
            for j in range(3 * per):
                stage_ref[j] = parts[j // per][:, (j % per) * LANES:(j % per + 1) * LANES]
            for r in range(dilation):
                for j in range(3 * per):
                    ref[r, :, j * LANES:(j + 1) * LANES] = stage_ref[
                        j, pl.ds(r, tm // dilation, stride=dilation), :].astype(BF16)


def _in_projection(x, mod4, nw, w_cat, layer, tables, qnw, knw, ones_bd, conv_w, widths, tm):
    bsz, t, d = x.shape
    cos_f, sin_a, sin_b = tables
    sww = widths[3] // 3
    row = lambda wd: pl.BlockSpec((None, tm, wd), lambda b, i: (b, i, 0))
    modspec = lambda k: pl.BlockSpec((None, None, 1, d), lambda b, i: (b, k, 0, 0))
    tabspec = pl.BlockSpec((tm, sww), lambda b, i: (i, 0))
    sw_specs = [pl.BlockSpec((None, dil, tm // dil, 3 * sww), lambda b, i: (b, 0, i, 0))
                for _, dil in SW_PATTERNS]
    sw_shapes = [jax.ShapeDtypeStruct((bsz, dil, t // dil, 3 * sww), BF16) for _, dil in SW_PATTERNS]
    f32_out = lambda wd: jax.ShapeDtypeStruct((bsz, t, wd), F32)
    return pl.pallas_call(
        functools.partial(_in_kernel, widths=widths),
        grid=(bsz, t // tm),
        in_specs=[
            row(d), modspec(0), modspec(1), _resident(nw.shape), _layer_resident(w_cat, layer),
            tabspec, tabspec, tabspec, _resident(qnw.shape), _resident(knw.shape),
            _resident(ones_bd.shape), _resident(conv_w.shape),
        ],
        out_specs=[pl.BlockSpec((None, widths[0] // LANES, tm, LANES), lambda b, i: (b, 0, i, 0)),
                   row(widths[1]), row(widths[2])] + sw_specs + [row(widths[6])],
        out_shape=[jax.ShapeDtypeStruct((bsz, widths[0] // LANES, t, LANES), F32),
                   f32_out(widths[1]), f32_out(widths[2])] + sw_shapes
        + [f32_out(widths[6])],
        scratch_shapes=[pltpu.VMEM((3 * sww // LANES, tm, LANES), F32),
                        pltpu.VMEM((widths[0] // LANES, tm + 8, LANES), F32)],
        compiler_params=_params(1, 1),
        name="in_projection",
    )(x, mod4, mod4, nw, w_cat, cos_f, sin_a, sin_b, qnw, knw, ones_bd, conv_w)


def _expand(x, e):
    x1 = x.astype(BF16)
    r = x - x1.astype(F32)
    x2 = r.astype(BF16)
    x3 = (r - x2.astype(F32)).astype(BF16)
    return _dot(x1, e) + _dot(x2, e) + _dot(x3, e)


def _block_diag(x, blk):
    c, n = x.shape
    zero = jnp.zeros((c, LANES), x.dtype)
    low = lax.broadcasted_iota(jnp.int32, (c, LANES), 1) < blk
    out = []
    for h in range(n // blk):
        g = (h * blk) // LANES
        piece = x[:, g * LANES:(g + 1) * LANES]
        if blk < LANES:
            piece = jnp.where(low if (h * blk) % LANES == 0 else jnp.logical_not(low), piece, zero)
        out.append(jnp.concatenate([piece if j == g else zero for j in range(n // LANES)], axis=1))
    return jnp.concatenate(out, axis=0)


def _dn_kernel(qkv_ref, z_ref, ab_ref, alog_ref, dtb_ref, onw_ref, eg64_ref, eg128_ref,
               eb128_ref, o_ref, q_s, qd_s, k_s, kb_s, kd_s, vb_s, wr_s, eg_s, gcat_s, s_ref,
               *, tt, chunk, heads):
    hd = DN_HEAD_DIM
    nseq = qkv_ref.shape[0]
    width = heads * hd
    cat = heads * chunk
    nchunks = tt // chunk
    assert hd == LANES and 2 * chunk == LANES

    @pl.when(pl.program_id(1) == 0)
    def _():
        s_ref[...] = jnp.zeros_like(s_ref)

    for b in range(nseq):
        ab = ab_ref[b]
        sp = ab + dtb_ref[...]
        softplus = jnp.maximum(sp, 0.0) + jnp.log1p(jnp.exp(-jnp.abs(sp)))
        g = -jnp.exp(alog_ref[...]) * softplus
        pos = lax.broadcasted_iota(jnp.int32, g.shape, 0) % chunk
        step = 1
        while step < chunk:
            g = g + jnp.where(pos >= step, pltpu.roll(g, step, 0), 0.0)
            step *= 2
        gcat_s[b] = _expand(g, eg64_ref[...])
        gexp = _expand(g, eg128_ref[...])
        bexp = _expand(_sigmoid(ab), eb128_ref[...])
        eg = jnp.exp(gexp)
        eg_s[b] = eg
        g_last = jnp.concatenate(
            [jnp.broadcast_to(gexp[(c + 1) * chunk - 1:(c + 1) * chunk, :], (chunk, width))
             for c in range(nchunks)], axis=0)
        kd_scale = jnp.exp(g_last - gexp)

        for h in range(heads):
            hc = slice(h * hd, (h + 1) * hd)
            q, k, v = qkv_ref[b, h], qkv_ref[b, heads + h], qkv_ref[b, 2 * heads + h]
            kb = k * bexp[:, hc]
            q_s[b, :, hc] = q.astype(BF16)
            qd_s[b, :, hc] = (q * eg[:, hc]).astype(BF16)
            k_s[b, :, hc] = k.astype(BF16)
            kb_s[b, :, hc] = kb.astype(BF16)
            kd_s[b, :, hc] = (k * kd_scale[:, hc]).astype(BF16)
            vb_s[b, :, hc] = (v * bexp[:, hc]).astype(BF16)
            wr_s[b, :, hc] = (kb * eg[:, hc]).astype(BF16)

    ri = lax.broadcasted_iota(jnp.int32, (chunk, cat), 0)
    li = lax.broadcasted_iota(jnp.int32, (chunk, cat), 1) % chunk
    causal = ri >= li
    strict = ri > li
    diag = ri == li
    eye_cat = jnp.where(diag, 1.0, 0.0).astype(F32)
    rows = [slice(c * chunk, (c + 1) * chunk) for c in range(nchunks)]
    units = [(b, c) for c in range(nchunks) for b in range(nseq)]
    nunits = len(units)

    a_all, a_qk = [], []
    for b, c in units:
        k_bd = _block_diag(k_s[b, rows[c], :], hd)
        both = _dot_nt(jnp.concatenate([kb_s[b, rows[c], :], q_s[b, rows[c], :]], axis=0), k_bd)
        gc = gcat_s[b, rows[c], :]
        g_row = jnp.sum(jnp.where(diag, gc, 0.0), axis=0, keepdims=True)
        decay = jnp.where(causal, jnp.exp(gc - g_row), 0.0)
        a_low = jnp.where(strict, both[:chunk] * decay, 0.0)
        a_qk.append((both[chunk:] * decay).astype(BF16))
        a_all.append(a_low)
    same = lambda s: (ri // s) == (li // s)
    in_base = same(DN_INV_BASE)
    inv, pw = [], []
    for i in range(nunits):
        d_f = jnp.where(in_base, a_all[i], 0.0)
        d_b = d_f.astype(BF16)
        inv.append(eye_cat - d_f)
        pw.append(_dot(d_b, _block_diag(d_b, chunk)))
    n_base = int(math.log2(DN_INV_BASE)) - 1
    for it in range(n_base):
        for i in range(nunits):
            pw_b = pw[i].astype(BF16)
            pw_bd = _block_diag(pw_b, chunk)
            if it + 1 < n_base:
                r = _dot(jnp.concatenate([inv[i].astype(BF16), pw_b], axis=0), pw_bd)
                inv[i] = inv[i] + r[:chunk]
                pw[i] = r[chunk:]
            else:
                inv[i] = inv[i] + _dot(inv[i].astype(BF16), pw_bd)
    size = DN_INV_BASE
    while size < chunk:
        off_diag = same(2 * size) & jnp.logical_not(same(size))
        for i in range(nunits):
            t_b = inv[i].astype(BF16)
            n_b = jnp.where(off_diag, a_all[i], 0.0).astype(BF16)
            t_n = _dot(t_b, _block_diag(n_b, chunk))
            inv[i] = inv[i] - _dot(t_n.astype(BF16), _block_diag(t_b, chunk))
        size *= 2
    hcols = [slice(h * hd, (h + 1) * hd) for h in range(heads)]
    hrows = [slice(h * chunk, (h + 1) * chunk) for h in range(heads)]
    stack = lambda ref, b, c: jnp.concatenate(
        [ref[b, rows[c], hcols[h]] for h in range(heads)], axis=0)
    u, w = [], []
    for i, (b, c) in enumerate(units):
        rhs = jnp.concatenate([stack(vb_s, b, c), stack(wr_s, b, c)], axis=1)
        uw = _dot(_block_diag(inv[i].astype(BF16), chunk), rhs)
        u.append(uw[:, :hd])
        w.append(uw[:, hd:].astype(BF16))

    state = [[s_ref[b, h] for h in range(heads)] for b in range(nseq)]
    for i, (b, c) in enumerate(units):
        res = [_dot(jnp.concatenate([w[i][hrows[h], :], qd_s[b, rows[c], hcols[h]]], axis=0),
                    state[b][h].astype(BF16)) for h in range(heads)]
        v_new = u[i] - jnp.concatenate([r[:chunk] for r in res], axis=0)
        v_new_b = v_new.astype(BF16)
        o = (jnp.concatenate([r[chunk:] for r in res], axis=0)
             + _dot(_block_diag(a_qk[i], chunk), v_new_b))
        last = (c + 1) * chunk - 1
        for h in range(heads):
            o_ref[b, rows[c], hcols[h]] = o[hrows[h], :]
            upd = _dot_tn(kd_s[b, rows[c], hcols[h]], v_new_b[hrows[h], :])
            state[b][h] = state[b][h] * eg_s[b, last:last + 1, hcols[h]] + upd
    for b in range(nseq):
        for h in range(heads):
            s_ref[b, h] = state[b][h]

    for b in range(nseq):
        for h in range(heads):
            o = o_ref[b, :, hcols[h]]
            on = o * lax.rsqrt(jnp.mean(o * o, axis=-1, keepdims=True) + NORM_EPS) * onw_ref[...]
            o_ref[b, :, hcols[h]] = on * _silu(z_ref[b, :, hcols[h]])


def _deltanet(qkv, z, ab, alog_pad, dtb_pad, onw, tt, nseq):
    bsz, nblk, t, _ = qkv.shape
    heads = nblk // 3
    width = heads * DN_HEAD_DIM
    chunk = DN_CHUNK
    cat = heads * chunk
    hh = jnp.arange(AB_PAD)[:, None]
    e_g64 = (hh == jnp.arange(cat)[None, :] // chunk).astype(BF16)
    e_g128 = (hh == jnp.arange(width)[None, :] // DN_HEAD_DIM).astype(BF16)
    e_b128 = (hh == heads + jnp.arange(width)[None, :] // DN_HEAD_DIM).astype(BF16)
    row = lambda wd: pl.BlockSpec((nseq, tt, wd), lambda b, i: (b, i, 0))
    return pl.pallas_call(
        functools.partial(_dn_kernel, tt=tt, chunk=chunk, heads=heads),
        grid=(bsz // nseq, t // tt),
        in_specs=[pl.BlockSpec((nseq, nblk, tt, LANES), lambda b, i: (b, 0, i, 0)),
                  row(width), row(AB_PAD),
                  _resident(alog_pad.shape), _resident(dtb_pad.shape), _resident(onw.shape),
                  _resident(e_g64.shape), _resident(e_g128.shape), _resident(e_b128.shape)],
        out_specs=row(width),
        out_shape=jax.ShapeDtypeStruct((bsz, t, width), F32),
        scratch_shapes=[pltpu.VMEM((nseq, tt, width), BF16)] * 7
        + [pltpu.VMEM((nseq, tt, width), F32), pltpu.VMEM((nseq, tt, cat), F32),
           pltpu.VMEM((nseq, heads, DN_HEAD_DIM, DN_HEAD_DIM), F32)],
        compiler_params=_params(1, 1),
        name="deltanet",
    )(qkv, z, ab, alog_pad, dtb_pad, onw, e_g64, e_g128, e_b128)


def _gm_kernel(uv_ref, lng_ref, lnb_ref, ws_ref, bias_ref, o_ref, *, tt, groups):
    width = groups * GM_GROUP_DIM
    x = uv_ref[...]
    zz = 0.5 * x * (1.0 + lax.erf(x * (2.0 ** -0.5)))
    u, v = zz[:, :width], zz[:, width:]
    mu = jnp.mean(v, axis=-1, keepdims=True)
    vc = v - mu
    var = jnp.mean(vc * vc, axis=-1, keepdims=True)
    vn = (vc * lax.rsqrt(var + NORM_EPS) * lng_ref[...] + lnb_ref[...]).astype(BF16)
    ii = lax.broadcasted_iota(jnp.int32, (GM_CHUNK, GM_CHUNK), 0)
    jj = lax.broadcasted_iota(jnp.int32, (GM_CHUNK, GM_CHUNK), 1)
    ws = [jnp.where(ii >= jj, ws_ref[g], 0.0).astype(BF16) for g in range(groups)]
    bias = bias_ref[...]
    for c in range(tt // GM_CHUNK):
        rows = slice(c * GM_CHUNK, (c + 1) * GM_CHUNK)
        parts = [_dot(ws[g], vn[rows, g * GM_GROUP_DIM:(g + 1) * GM_GROUP_DIM])
                 for g in range(groups)]
        o_ref[rows, :] = u[rows, :] * (jnp.concatenate(parts, axis=1) + bias)


def _spatial_gating(uv, ln_g, ln_b, w_s, bias_full, tt):
    bsz, t, w2 = uv.shape
    width = w2 // 2
    groups = width // GM_GROUP_DIM
    return pl.pallas_call(
        functools.partial(_gm_kernel, tt=tt, groups=groups),
        grid=(bsz, t // tt),
        in_specs=[pl.BlockSpec((None, tt, w2), lambda b, i: (b, i, 0)),
                  _resident(ln_g.shape), _resident(ln_b.shape), _resident(w_s.shape),
                  _resident(bias_full.shape)],
        out_specs=pl.BlockSpec((None, tt, width), lambda b, i: (b, i, 0)),
        out_shape=jax.ShapeDtypeStruct((bsz, t, width), F32),
        compiler_params=_params(2),
        name="spatial_gating",
    )(uv, ln_g, ln_b, w_s, bias_full)


def _sw_kernel(q_ref, k_ref, v_ref, kp_ref, vp_ref, o_ref, lse_ref, *, dilation, heads, span):
    blk = SW_BLOCK
    hd = SW_HEAD_DIM
    nk = 2 * blk
    nqb = q_ref.shape[1] // blk
    assert 2 * hd == LANES and heads % 2 == 0
    a = lax.broadcasted_iota(jnp.int32, (2 * blk, nk), 0) % blk
    c = lax.broadcasted_iota(jnp.int32, (2 * blk, nk), 1)
    dist = blk + a - c
    in_window = (dist >= 0) & (dist <= span)
    own_block = c >= blk
    low = lax.broadcasted_iota(jnp.int32, (blk, LANES), 1) < hd
    ones = jnp.ones((nk, LANES), BF16)
    units = [(r, qb, j) for r in range(dilation) for qb in range(nqb) for j in range(heads // 2)]
    group = 4
    for g0 in range(0, len(units), group):
        grp = units[g0:g0 + group]
        ss, vos = [], []
        for r, qb, j in grp:
            rows = slice(qb * blk, (qb + 1) * blk)
            lanes = slice(j * LANES, (j + 1) * LANES)
            if qb == 0:
                kk = jnp.concatenate([kp_ref[r, :, lanes], k_ref[r, rows, lanes]], axis=0)
                vv = jnp.concatenate([vp_ref[r, :, lanes], v_ref[r, rows, lanes]], axis=0)
            else:
                kk = k_ref[r, (qb - 1) * blk:(qb + 1) * blk, lanes]
                vv = v_ref[r, (qb - 1) * blk:(qb + 1) * blk, lanes]
            q = q_ref[r, rows, lanes]
            zero = jnp.zeros_like(q)
            q2 = jnp.concatenate([jnp.where(low, q, zero), jnp.where(low, zero, q)], axis=0)
            s = _dot_nt(q2, kk)
            has_prev = (pl.program_id(1) * nqb + qb) > 0
            ss.append(jnp.where(in_window & (own_block | has_prev), s, -jnp.inf))
            vos.append(jnp.concatenate([vv, ones], axis=1))
        ms, ps = [], []
        for s in ss:
            m = jnp.max(s, axis=-1, keepdims=True)
            ps.append(jnp.exp(s - m).astype(BF16))
            ms.append(jnp.where(low, jnp.broadcast_to(m[:blk], (blk, LANES)),
                                jnp.broadcast_to(m[blk:], (blk, LANES))))
        for (r, qb, j), p, vo, m in zip(grp, ps, vos, ms):
            pv = _dot(p, vo)
            o = jnp.where(low, pv[:blk, :LANES], pv[blk:, :LANES])
            l = jnp.where(low, pv[:blk, LANES:], pv[blk:, LANES:])
            if dilation == 1:
                tok = slice(qb * blk, (qb + 1) * blk)
            else:
                tok = pl.ds(r + dilation * qb * blk, blk, stride=dilation)
            o_ref[j, tok, :] = o / l
            lse_ref[j, tok, :] = m + jnp.log(l)


def _window_attention(qkv, window, dilation):
    bsz, _, length, w3 = qkv.shape
    t = length * dilation
    width = w3 // 3
    heads = width // SW_HEAD_DIM
    span = window // dilation
    assert span <= SW_BLOCK and length % SW_BLOCK == 0
    nb = max(2, 8 // dilation)
    rl = nb * SW_BLOCK
    cur = lambda j: pl.BlockSpec((None, dilation, rl, width), lambda b, i: (b, 0, i, j))
    prev = lambda j: pl.BlockSpec((None, dilation, SW_BLOCK, width),
                                  lambda b, i: (b, 0, jnp.maximum(i * nb - 1, 0), j))
    out = pl.BlockSpec((None, width // LANES, rl * dilation, LANES), lambda b, i: (b, 0, i, 0))
    return pl.pallas_call(
        functools.partial(_sw_kernel, dilation=dilation, heads=heads, span=span),
        grid=(bsz, length // rl),
        in_specs=[cur(0), cur(1), cur(2), prev(1), prev(2)],
        out_specs=[out, out],
        out_shape=[jax.ShapeDtypeStruct((bsz, width // LANES, t, LANES), F32)] * 2,
        compiler_params=_params(2),
        name=f"window_attention_d{dilation}",
    )(qkv, qkv, qkv, qkv, qkv)


def _out_kernel(x_ref, ya_ref, yb_ref, o1_ref, l1_ref, o2_ref, l2_ref, o3_ref, l3_ref,
                gate1_ref, shift2_ref, scale2_ref, gate2_ref, fnw_ref,
                wout_ref, wfi_ref, wfo_ref, out_ref, *, ffn_chunks):
    ycs = []
    for j in range(o1_ref.shape[0]):
        l1, l2, l3 = l1_ref[j], l2_ref[j], l3_ref[j]
        m = jnp.maximum(jnp.maximum(l1, l2), l3)
        e1, e2, e3 = jnp.exp(l1 - m), jnp.exp(l2 - m), jnp.exp(l3 - m)
        yc = (e1 * o1_ref[j] + e2 * o2_ref[j] + e3 * o3_ref[j]) / (e1 + e2 + e3)
        ycs.append(yc.astype(BF16))
    y = jnp.concatenate([ya_ref[...].astype(BF16), yb_ref[...].astype(BF16)] + ycs, axis=1)
    x1 = x_ref[...] + gate1_ref[...] * _dot(y, wout_ref[...])
    ms = jnp.mean(x1 * x1, axis=-1, keepdims=True)
    h = x1 * lax.rsqrt(ms + NORM_EPS) * (fnw_ref[...] * (1.0 + scale2_ref[...])) + shift2_ref[...]
    hb = h.astype(BF16)
    hidden = wfo_ref.shape[0]
    fc = hidden // ffn_chunks
    acc = None
    for j in range(ffn_chunks):
        gate = _dot(hb, wfi_ref[:, j * fc:(j + 1) * fc])
        up = _dot(hb, wfi_ref[:, hidden + j * fc:hidden + (j + 1) * fc])
        part = _dot((_silu(gate) * up).astype(BF16), wfo_ref[j * fc:(j + 1) * fc, :])
        acc = part if acc is None else acc + part
    out_ref[...] = x1 + gate2_ref[...] * acc


def _out_ffn(x, ya, yb, attn, mod4, fnw, wout, wfi, wfo, layer, tm):
    bsz, t, d = x.shape
    row = lambda wd: pl.BlockSpec((None, tm, wd), lambda b, i: (b, i, 0))
    modspec = lambda k: pl.BlockSpec((None, None, 1, d), lambda b, i: (b, k, 0, 0))
    attn_flat = [a for pair in attn for a in pair]
    return pl.pallas_call(
        functools.partial(_out_kernel, ffn_chunks=1),
        grid=(bsz, t // tm),
        in_specs=[row(d), row(ya.shape[-1]), row(yb.shape[-1])]
        + [pl.BlockSpec((None, a.shape[1], tm, LANES), lambda b, i: (b, 0, i, 0)) for a in attn_flat]
        + [modspec(2), modspec(3), modspec(4), modspec(5), _resident(fnw.shape),
           _layer_resident(wout, layer), _layer_resident(wfi, layer),
           _layer_resident(wfo, layer)],
        out_specs=row(d),
        out_shape=jax.ShapeDtypeStruct((bsz, t, d), F32),
        compiler_params=_params(2),
        name="out_ffn",
    )(x, ya, yb, *attn_flat, mod4, mod4, mod4, mod4, fnw, wout, wfi, wfo)


def _rotary_tables(t, heads):
    half = ROPE_DIM // 2
    inv = ROPE_THETA ** (-jnp.arange(0, ROPE_DIM, 2, dtype=F32) / ROPE_DIM)
    ang = jnp.arange(t, dtype=F32)[:, None] * inv[None, :]
    cos, sin = jnp.cos(ang), jnp.sin(ang)
    rest = SW_HEAD_DIM - ROPE_DIM
    ones = jnp.ones((t, rest), F32)
    zeros = jnp.zeros((t, rest), F32)
    zh = jnp.zeros((t, half), F32)
    cos_f = jnp.concatenate([cos, cos, ones], axis=1)
    sin_a = jnp.concatenate([-sin, zh, zeros], axis=1)
    sin_b = jnp.concatenate([zh, sin, zeros], axis=1)
    tile = lambda a: jnp.tile(a, (1, heads))
    return tile(cos_f), tile(sin_a), tile(sin_b)


def kernel(x, c, w_mod, b_mod, mix_norm_w, ffn_norm_w, w_in, w_out, dn_conv_w, dn_a_log,
           dn_dt_bias, dn_out_norm_w, gm_ln_g, gm_ln_b, gm_w_s, gm_b_s, sw_q_norm_w,
           sw_k_norm_w, w_ffn_in, w_ffn_out):
    bsz, t, d = x.shape
    nl = w_mod.shape[0]
    dn_heads = dn_a_log.shape[1]
    dn_width = dn_heads * DN_HEAD_DIM
    gm_groups = gm_w_s.shape[1]
    gm_width = gm_groups * GM_GROUP_DIM
    npat = len(SW_PATTERNS)
    sw_width = (w_in.shape[2] - 4 * dn_width - 2 * dn_heads - 2 * gm_width) // (3 * npat)
    sw_heads = sw_width // SW_HEAD_DIM
    assert 2 * dn_heads <= AB_PAD

    o_z = 3 * dn_width
    o_a = o_z + dn_width
    o_gm = o_a + 2 * dn_heads
    o_sw = o_gm + 2 * gm_width
    widths = (3 * dn_width, dn_width, 2 * gm_width) + (3 * sw_width,) * npat + (AB_PAD,)

    mod = _modulation(c, w_mod, b_mod)
    tables = _rotary_tables(t, sw_heads)
    ii = jnp.arange(sw_width) // SW_HEAD_DIM
    ones_bd = ((ii[:, None] == ii[None, :]) * (1.0 / SW_HEAD_DIM)).astype(BF16)
    tm = 512
    ab_cols = jnp.zeros((nl, d, AB_PAD), F32).at[:, :, :2 * dn_heads].set(w_in[:, :, o_a:o_gm])
    w_cat = jnp.concatenate(
        [w_in[:, :, :o_a], w_in[:, :, o_gm:o_sw], w_in[:, :, o_sw:], ab_cols], axis=2).astype(BF16)
    w_out_b, w_ffn_in_b, w_ffn_out_b = (w.astype(BF16) for w in (w_out, w_ffn_in, w_ffn_out))

    for layer in range(nl):
        mod4 = mod[layer].reshape(bsz, 6, 1, d)
        dn_qkv, dn_z, gm_uv, sw1, sw2, sw3, dn_ab = _in_projection(
            x, mod4, mix_norm_w[layer].reshape(1, d), w_cat, layer, tables,
            jnp.tile(sw_q_norm_w[layer] * (SW_HEAD_DIM ** -0.5), sw_heads).reshape(1, sw_width),
            jnp.tile(sw_k_norm_w[layer], sw_heads).reshape(1, sw_width),
            ones_bd, dn_conv_w[layer], widths, tm)

        pad_heads = lambda v: jnp.zeros((1, AB_PAD), F32).at[0, :dn_heads].set(v)
        y_a = _deltanet(dn_qkv, dn_z, dn_ab, pad_heads(dn_a_log[layer]),
                        pad_heads(dn_dt_bias[layer]), dn_out_norm_w[layer].reshape(1, DN_HEAD_DIM),
                        tt=256, nseq=4 if bsz % 4 == 0 else 1)
        bias_full = jnp.repeat(gm_b_s[layer].T, GM_GROUP_DIM, axis=1)
        y_b = _spatial_gating(gm_uv, gm_ln_g[layer].reshape(1, gm_width),
                              gm_ln_b[layer].reshape(1, gm_width), gm_w_s[layer], bias_full,
                              tt=1024)
        attn = [_window_attention(sw, window, dilation)
                for sw, (window, dilation) in zip((sw1, sw2, sw3), SW_PATTERNS)]
        x = _out_ffn(x, y_a, y_b, attn, mod4, ffn_norm_w[layer].reshape(1, d),
                     w_out_b, w_ffn_in_b, w_ffn_out_b, layer, tm)
    return x
```

```python
import functools
import math

import jax
import jax.numpy as jnp
from jax import lax
from jax.experimental import pallas as pl
from jax.experimental.pallas import tpu as pltpu

F32 = jnp.float32
BF16 = jnp.bfloat16

NORM_EPS = 1e-6
DN_HEAD_DIM = 128
DN_CONV = 4
DN_CHUNK = 64
DN_INV_BASE = 16
GM_GROUP_DIM = 64
GM_CHUNK = 128
SW_HEAD_DIM = 64
SW_PATTERNS = ((128, 1), (512, 4), (2048, 16))
SW_BLOCK = 128
ROPE_THETA = 500000.0
ROPE_DIM = SW_HEAD_DIM // 4
LANES = 128
AB_PAD = LANES

VMEM_LIMIT_BYTES = 56 * 1024 * 1024


def _params(n_parallel, n_arbitrary=0):
    return pltpu.CompilerParams(
        dimension_semantics=("parallel",) * n_parallel + ("arbitrary",) * n_arbitrary,
        vmem_limit_bytes=VMEM_LIMIT_BYTES,
    )


def _resident(shape):
    zeros = (0,) * len(shape)
    return pl.BlockSpec(shape, lambda *_: zeros, pipeline_mode=pl.Buffered(1))


def _layer_resident(stacked, layer):
    zeros = (0,) * (stacked.ndim - 1)
    return pl.BlockSpec((None,) + stacked.shape[1:], lambda *_: (layer,) + zeros,
                        pipeline_mode=pl.Buffered(1))


def _sigmoid(x):
    return 1.0 / (1.0 + jnp.exp(-x))


def _silu(x):
    return x * _sigmoid(x)


def _dot(a, b):
    return jnp.dot(a, b, preferred_element_type=F32)


def _dot_nt(a, b):
    return lax.dot_general(a, b, (((1,), (1,)), ((), ())), preferred_element_type=F32)


def _dot_tn(a, b):
    return lax.dot_general(a, b, (((0,), (0,)), ((), ())), preferred_element_type=F32)


def _mod_kernel(c_ref, w_ref, b_ref, o_ref):
    c = c_ref[...]
    o_ref[...] = _dot(_silu(c), w_ref[...]) + b_ref[...]


def _modulation(c, w_mod, b_mod):
    nl, d, e = w_mod.shape
    bsz = c.shape[0]
    tn = d
    return pl.pallas_call(
        _mod_kernel,
        grid=(nl, e // tn),
        in_specs=[
            pl.BlockSpec((bsz, d), lambda l, j: (0, 0)),
            pl.BlockSpec((None, d, tn), lambda l, j: (l, 0, j)),
            pl.BlockSpec((None, 1, tn), lambda l, j: (l, 0, j)),
        ],
        out_specs=pl.BlockSpec((None, bsz, tn), lambda l, j: (l, 0, j)),
        out_shape=jax.ShapeDtypeStruct((nl, bsz, e), F32),
        compiler_params=_params(2),
        name="modulation",
    )(c, w_mod, b_mod.reshape(nl, 1, e))


def _group_mean_sq(x, ones_bd, group):
    del group
    return _dot((x * x).astype(BF16), ones_bd)


def _rotary(x, cos_f, sin_a, sin_b):
    n = x.shape[-1]
    half = ROPE_DIM // 2
    return x * cos_f + pltpu.roll(x, n - half, 1) * sin_a + pltpu.roll(x, half, 1) * sin_b


def _in_kernel(x_ref, shift_ref, scale_ref, nw_ref, w_ref, cos_ref, sina_ref, sinb_ref,
               qnw_ref, knw_ref, bd_ref, cw_ref,
               dn_ref, z_ref, gm_ref, sw1_ref, sw2_ref, sw3_ref, ab_ref, stage_ref, hist_ref,
               *, widths):
    tm = x_ref.shape[0]
    x = x_ref[...]
    ms = jnp.mean(x * x, axis=-1, keepdims=True)
    h = x * lax.rsqrt(ms + NORM_EPS) * (nw_ref[...] * (1.0 + scale_ref[...])) + shift_ref[...]
    hb = h.astype(BF16)
    offs = [0]
    for wd in widths:
        offs.append(offs[-1] + wd)
    plain = ((z_ref, 1), (gm_ref, 2), (ab_ref, 6))
    for ref, idx in plain:
        ref[...] = _dot(hb, w_ref[:, offs[idx]:offs[idx + 1]])

    pad = hist_ref.shape[1] - tm
    nblk = hist_ref.shape[0]

    @pl.when(pl.program_id(1) == 0)
    def _():
        hist_ref[:, 0:pad, :] = jnp.zeros((nblk, pad, LANES), F32)

    @pl.when(pl.program_id(1) > 0)
    def _():
        hist_ref[:, 0:pad, :] = hist_ref[:, tm:tm + pad, :]

    raw = _dot(hb, w_ref[:, offs[0]:offs[1]])
    for cb in range(nblk):
        hist_ref[cb, pad:pad + tm, :] = raw[:, cb * LANES:(cb + 1) * LANES]
    phases = 8
    per = tm // phases
    for cb in range(nblk):
        cols = slice(cb * LANES, (cb + 1) * LANES)
        taps = [cw_ref[tap:tap + 1, cols] for tap in range(DN_CONV)]
        x = {j: hist_ref[cb, pl.ds(pad + j, per, stride=phases), :]
             for j in range(1 - DN_CONV, phases)}
        for s in range(phases):
            y = taps[DN_CONV - 1] * x[s]
            for tap in range(DN_CONV - 1):
                y = y + taps[tap] * x[s - (DN_CONV - 1 - tap)]
            y = _silu(y)
            if cb < 2 * nblk // 3:
                y = y * lax.rsqrt(jnp.sum(y * y, axis=-1, keepdims=True) + NORM_EPS)
            if cb < nblk // 3:
                y = y * (DN_HEAD_DIM ** -0.5)
            dn_ref[cb, pl.ds(s, per, stride=phases), :] = y
    sww = widths[3] // 3
    cos_f, sin_a, sin_b = cos_ref[...], sina_ref[...], sinb_ref[...]
    bd = bd_ref[...]
    for ref, idx in ((sw1_ref, 3), (sw2_ref, 4), (sw3_ref, 5)):
        raw = _dot(hb, w_ref[:, offs[idx]:offs[idx + 1]])
        q, k, v = raw[:, :sww], raw[:, sww:2 * sww], raw[:, 2 * sww:]
        qn = q * lax.rsqrt(_group_mean_sq(q, bd, SW_HEAD_DIM) + NORM_EPS) * qnw_ref[...]
        kn = k * lax.rsqrt(_group_mean_sq(k, bd, SW_HEAD_DIM) + NORM_EPS) * knw_ref[...]
        qn = _rotary(qn, cos_f, sin_a, sin_b)
        kn = _rotary(kn, cos_f, sin_a, sin_b)
        dilation = ref.shape[0]
        if dilation == 1:
            ref[0, :, :sww] = qn.astype(BF16)
            ref[0, :, sww:2 * sww] = kn.astype(BF16)
            ref[0, :, 2 * sww:] = v.astype(BF16)
        else:
            parts = (qn, kn, v)
            per = sww // LANES
            for j in range(3 * per):
                stage_ref[j] = parts[j // per][:, (j % per) * LANES:(j % per + 1) * LANES]
            for r in range(dilation):
                for j in range(3 * per):
                    ref[r, :, j * LANES:(j + 1) * LANES] = stage_ref[
                        j, pl.ds(r, tm // dilation, stride=dilation), :].astype(BF16)


def _in_projection(x, mod4, nw, w_cat, layer, tables, qnw, knw, ones_bd, conv_w, widths, tm):
    bsz, t, d = x.shape
    cos_f, sin_a, sin_b = tables
    sww = widths[3] // 3
    row = lambda wd: pl.BlockSpec((None, tm, wd), lambda b, i: (b, i, 0))
    modspec = lambda k: pl.BlockSpec((None, None, 1, d), lambda b, i: (b, k, 0, 0))
    tabspec = pl.BlockSpec((tm, sww), lambda b, i: (i, 0))
    sw_specs = [pl.BlockSpec((None, dil, tm // dil, 3 * sww), lambda b, i: (b, 0, i, 0))
                for _, dil in SW_PATTERNS]
    sw_shapes = [jax.ShapeDtypeStruct((bsz, dil, t // dil, 3 * sww), BF16) for _, dil in SW_PATTERNS]
    f32_out = lambda wd: jax.ShapeDtypeStruct((bsz, t, wd), F32)
    return pl.pallas_call(
        functools.partial(_in_kernel, widths=widths),
        grid=(bsz, t // tm),
        in_specs=[
            row(d), modspec(0), modspec(1), _resident(nw.shape), _layer_resident(w_cat, layer),
            tabspec, tabspec, tabspec, _resident(qnw.shape), _resident(knw.shape),
            _resident(ones_bd.shape), _resident(conv_w.shape),
        ],
        out_specs=[pl.BlockSpec((None, widths[0] // LANES, tm, LANES), lambda b, i: (b, 0, i, 0)),
                   row(widths[1]), row(widths[2])] + sw_specs + [row(widths[6])],
        out_shape=[jax.ShapeDtypeStruct((bsz, widths[0] // LANES, t, LANES), F32),
                   f32_out(widths[1]), f32_out(widths[2])] + sw_shapes
        + [f32_out(widths[6])],
        scratch_shapes=[pltpu.VMEM((3 * sww // LANES, tm, LANES), F32),
                        pltpu.VMEM((widths[0] // LANES, tm + 8, LANES), F32)],
        compiler_params=_params(1, 1),
        name="in_projection",
    )(x, mod4, mod4, nw, w_cat, cos_f, sin_a, sin_b, qnw, knw, ones_bd, conv_w)


def _expand(x, e):
    x1 = x.astype(BF16)
    r = x - x1.astype(F32)
    x2 = r.astype(BF16)
    x3 = (r - x2.astype(F32)).astype(BF16)
    return _dot(x1, e) + _dot(x2, e) + _dot(x3, e)


def _block_diag(x, blk):
    c, n = x.shape
    zero = jnp.zeros((c, LANES), x.dtype)
    low = lax.broadcasted_iota(jnp.int32, (c, LANES), 1) < blk
    out = []
    for h in range(n // blk):
        g = (h * blk) // LANES
        piece = x[:, g * LANES:(g + 1) * LANES]
        if blk < LANES:
            piece = jnp.where(low if (h * blk) % LANES == 0 else jnp.logical_not(low), piece, zero)
        out.append(jnp.concatenate([piece if j == g else zero for j in range(n // LANES)], axis=1))
    return jnp.concatenate(out, axis=0)


def _dn_kernel(qkv_ref, z_ref, ab_ref, alog_ref, dtb_ref, onw_ref, eg64_ref, eg128_ref,
               eb128_ref, o_ref, q_s, qd_s, k_s, kb_s, kd_s, vb_s, wr_s, eg_s, gcat_s, s_ref,
               *, tt, chunk, heads):
    hd = DN_HEAD_DIM
    nseq = qkv_ref.shape[0]
    width = heads * hd
    cat = heads * chunk
    nchunks = tt // chunk
    assert hd == LANES and 2 * chunk == LANES

    @pl.when(pl.program_id(1) == 0)
    def _():
        s_ref[...] = jnp.zeros_like(s_ref)

    for b in range(nseq):
        ab = ab_ref[b]
        sp = ab + dtb_ref[...]
        softplus = jnp.maximum(sp, 0.0) + jnp.log1p(jnp.exp(-jnp.abs(sp)))
        g = -jnp.exp(alog_ref[...]) * softplus
        pos = lax.broadcasted_iota(jnp.int32, g.shape, 0) % chunk
        step = 1
        while step < chunk:
            g = g + jnp.where(pos >= step, pltpu.roll(g, step, 0), 0.0)
            step *= 2
        gcat_s[b] = _expand(g, eg64_ref[...])
        gexp = _expand(g, eg128_ref[...])
        bexp = _expand(_sigmoid(ab), eb128_ref[...])
        eg = jnp.exp(gexp)
        eg_s[b] = eg
        g_last = jnp.concatenate(
            [jnp.broadcast_to(gexp[(c + 1) * chunk - 1:(c + 1) * chunk, :], (chunk, width))
             for c in range(nchunks)], axis=0)
        kd_scale = jnp.exp(g_last - gexp)

        for h in range(heads):
            hc = slice(h * hd, (h + 1) * hd)
            q, k, v = qkv_ref[b, h], qkv_ref[b, heads + h], qkv_ref[b, 2 * heads + h]
            kb = k * bexp[:, hc]
            q_s[b, :, hc] = q.astype(BF16)
            qd_s[b, :, hc] = (q * eg[:, hc]).astype(BF16)
            k_s[b, :, hc] = k.astype(BF16)
            kb_s[b, :, hc] = kb.astype(BF16)
            kd_s[b, :, hc] = (k * kd_scale[:, hc]).astype(BF16)
            vb_s[b, :, hc] = (v * bexp[:, hc]).astype(BF16)
            wr_s[b, :, hc] = (kb * eg[:, hc]).astype(BF16)

    ri = lax.broadcasted_iota(jnp.int32, (chunk, cat), 0)
    li = lax.broadcasted_iota(jnp.int32, (chunk, cat), 1) % chunk
    causal = ri >= li
    strict = ri > li
    diag = ri == li
    eye_cat = jnp.where(diag, 1.0, 0.0).astype(F32)
    rows = [slice(c * chunk, (c + 1) * chunk) for c in range(nchunks)]
    units = [(b, c) for c in range(nchunks) for b in range(nseq)]
    nunits = len(units)

    a_all, a_qk = [], []
    for b, c in units:
        k_bd = _block_diag(k_s[b, rows[c], :], hd)
        both = _dot_nt(jnp.concatenate([kb_s[b, rows[c], :], q_s[b, rows[c], :]], axis=0), k_bd)
        gc = gcat_s[b, rows[c], :]
        g_row = jnp.sum(jnp.where(diag, gc, 0.0), axis=0, keepdims=True)
        decay = jnp.where(causal, jnp.exp(gc - g_row), 0.0)
        a_low = jnp.where(strict, both[:chunk] * decay, 0.0)
        a_qk.append((both[chunk:] * decay).astype(BF16))
        a_all.append(a_low)
    same = lambda s: (ri // s) == (li // s)
    in_base = same(DN_INV_BASE)
    inv, pw = [], []
    for i in range(nunits):
        d_f = jnp.where(in_base, a_all[i], 0.0)
        d_b = d_f.astype(BF16)
        inv.append(eye_cat - d_f)
        pw.append(_dot(d_b, _block_diag(d_b, chunk)))
    n_base = int(math.log2(DN_INV_BASE)) - 1
    for it in range(n_base):
        for i in range(nunits):
            pw_b = pw[i].astype(BF16)
            pw_bd = _block_diag(pw_b, chunk)
            if it + 1 < n_base:
                r = _dot(jnp.concatenate([inv[i].astype(BF16), pw_b], axis=0), pw_bd)
                inv[i] = inv[i] + r[:chunk]
                pw[i] = r[chunk:]
            else:
                inv[i] = inv[i] + _dot(inv[i].astype(BF16), pw_bd)
    size = DN_INV_BASE
    while size < chunk:
        off_diag = same(2 * size) & jnp.logical_not(same(size))
        for i in range(nunits):
            t_b = inv[i].astype(BF16)
            n_b = jnp.where(off_diag, a_all[i], 0.0).astype(BF16)
            t_n = _dot(t_b, _block_diag(n_b, chunk))
            inv[i] = inv[i] - _dot(t_n.astype(BF16), _block_diag(t_b, chunk))
        size *= 2
    hcols = [slice(h * hd, (h + 1) * hd) for h in range(heads)]
    hrows = [slice(h * chunk, (h + 1) * chunk) for h in range(heads)]
    stack = lambda ref, b, c: jnp.concatenate(
        [ref[b, rows[c], hcols[h]] for h in range(heads)], axis=0)
    u, w = [], []
    for i, (b, c) in enumerate(units):
        rhs = jnp.concatenate([stack(vb_s, b, c), stack(wr_s, b, c)], axis=1)
        uw = _dot(_block_diag(inv[i].astype(BF16), chunk), rhs)
        u.append(uw[:, :hd])
        w.append(uw[:, hd:].astype(BF16))

    state = [[s_ref[b, h] for h in range(heads)] for b in range(nseq)]
    for i, (b, c) in enumerate(units):
        res = [_dot(jnp.concatenate([w[i][hrows[h], :], qd_s[b, rows[c], hcols[h]]], axis=0),
                    state[b][h].astype(BF16)) for h in range(heads)]
        v_new = u[i] - jnp.concatenate([r[:chunk] for r in res], axis=0)
        v_new_b = v_new.astype(BF16)
        o = (jnp.concatenate([r[chunk:] for r in res], axis=0)
             + _dot(_block_diag(a_qk[i], chunk), v_new_b))
        last = (c + 1) * chunk - 1
        for h in range(heads):
            o_ref[b, rows[c], hcols[h]] = o[hrows[h], :]
            upd = _dot_tn(kd_s[b, rows[c], hcols[h]], v_new_b[hrows[h], :])
            state[b][h] = state[b][h] * eg_s[b, last:last + 1, hcols[h]] + upd
    for b in range(nseq):
        for h in range(heads):
            s_ref[b, h] = state[b][h]

    for b in range(nseq):
        for h in range(heads):
            o = o_ref[b, :, hcols[h]]
            on = o * lax.rsqrt(jnp.mean(o * o, axis=-1, keepdims=True) + NORM_EPS) * onw_ref[...]
            o_ref[b, :, hcols[h]] = on * _silu(z_ref[b, :, hcols[h]])


def _deltanet(qkv, z, ab, alog_pad, dtb_pad, onw, tt, nseq):
    bsz, nblk, t, _ = qkv.shape
    heads = nblk // 3
    width = heads * DN_HEAD_DIM
    chunk = DN_CHUNK
    cat = heads * chunk
    hh = jnp.arange(AB_PAD)[:, None]
    e_g64 = (hh == jnp.arange(cat)[None, :] // chunk).astype(BF16)
    e_g128 = (hh == jnp.arange(width)[None, :] // DN_HEAD_DIM).astype(BF16)
    e_b128 = (hh == heads + jnp.arange(width)[None, :] // DN_HEAD_DIM).astype(BF16)
    row = lambda wd: pl.BlockSpec((nseq, tt, wd), lambda b, i: (b, i, 0))
    return pl.pallas_call(
        functools.partial(_dn_kernel, tt=tt, chunk=chunk, heads=heads),
        grid=(bsz // nseq, t // tt),
        in_specs=[pl.BlockSpec((nseq, nblk, tt, LANES), lambda b, i: (b, 0, i, 0)),
                  row(width), row(AB_PAD),
                  _resident(alog_pad.shape), _resident(dtb_pad.shape), _resident(onw.shape),
                  _resident(e_g64.shape), _resident(e_g128.shape), _resident(e_b128.shape)],
        out_specs=row(width),
        out_shape=jax.ShapeDtypeStruct((bsz, t, width), F32),
        scratch_shapes=[pltpu.VMEM((nseq, tt, width), BF16)] * 7
        + [pltpu.VMEM((nseq, tt, width), F32), pltpu.VMEM((nseq, tt, cat), F32),
           pltpu.VMEM((nseq, heads, DN_HEAD_DIM, DN_HEAD_DIM), F32)],
        compiler_params=_params(1, 1),
        name="deltanet",
    )(qkv, z, ab, alog_pad, dtb_pad, onw, e_g64, e_g128, e_b128)


def _gm_kernel(uv_ref, lng_ref, lnb_ref, ws_ref, bias_ref, o_ref, *, tt, groups):
    width = groups * GM_GROUP_DIM
    x = uv_ref[...]
    zz = 0.5 * x * (1.0 + lax.erf(x * (2.0 ** -0.5)))
    u, v = zz[:, :width], zz[:, width:]
    mu = jnp.mean(v, axis=-1, keepdims=True)
    vc = v - mu
    var = jnp.mean(vc * vc, axis=-1, keepdims=True)
    vn = (vc * lax.rsqrt(var + NORM_EPS) * lng_ref[...] + lnb_ref[...]).astype(BF16)
    ii = lax.broadcasted_iota(jnp.int32, (GM_CHUNK, GM_CHUNK), 0)
    jj = lax.broadcasted_iota(jnp.int32, (GM_CHUNK, GM_CHUNK), 1)
    ws = [jnp.where(ii >= jj, ws_ref[g], 0.0).astype(BF16) for g in range(groups)]
    bias = bias_ref[...]
    for c in range(tt // GM_CHUNK):
        rows = slice(c * GM_CHUNK, (c + 1) * GM_CHUNK)
        parts = [_dot(ws[g], vn[rows, g * GM_GROUP_DIM:(g + 1) * GM_GROUP_DIM])
                 for g in range(groups)]
        o_ref[rows, :] = u[rows, :] * (jnp.concatenate(parts, axis=1) + bias)


def _spatial_gating(uv, ln_g, ln_b, w_s, bias_full, tt):
    bsz, t, w2 = uv.shape
    width = w2 // 2
    groups = width // GM_GROUP_DIM
    return pl.pallas_call(
        functools.partial(_gm_kernel, tt=tt, groups=groups),
        grid=(bsz, t // tt),
        in_specs=[pl.BlockSpec((None, tt, w2), lambda b, i: (b, i, 0)),
                  _resident(ln_g.shape), _resident(ln_b.shape), _resident(w_s.shape),
                  _resident(bias_full.shape)],
        out_specs=pl.BlockSpec((None, tt, width), lambda b, i: (b, i, 0)),
        out_shape=jax.ShapeDtypeStruct((bsz, t, width), F32),
        compiler_params=_params(2),
        name="spatial_gating",
    )(uv, ln_g, ln_b, w_s, bias_full)


def _sw_kernel(q_ref, k_ref, v_ref, kp_ref, vp_ref, o_ref, lse_ref, *, dilation, heads, span):
    blk = SW_BLOCK
    hd = SW_HEAD_DIM
    nk = 2 * blk
    nqb = q_ref.shape[1] // blk
    assert 2 * hd == LANES and heads % 2 == 0
    a = lax.broadcasted_iota(jnp.int32, (2 * blk, nk), 0) % blk
    c = lax.broadcasted_iota(jnp.int32, (2 * blk, nk), 1)
    dist = blk + a - c
    in_window = (dist >= 0) & (dist <= span)
    own_block = c >= blk
    low = lax.broadcasted_iota(jnp.int32, (blk, LANES), 1) < hd
    ones = jnp.ones((nk, LANES), BF16)
    units = [(r, qb, j) for r in range(dilation) for qb in range(nqb) for j in range(heads // 2)]
    group = 4
    for g0 in range(0, len(units), group):
        grp = units[g0:g0 + group]
        ss, vos = [], []
        for r, qb, j in grp:
            rows = slice(qb * blk, (qb + 1) * blk)
            lanes = slice(j * LANES, (j + 1) * LANES)
            if qb == 0:
                kk = jnp.concatenate([kp_ref[r, :, lanes], k_ref[r, rows, lanes]], axis=0)
                vv = jnp.concatenate([vp_ref[r, :, lanes], v_ref[r, rows, lanes]], axis=0)
            else:
                kk = k_ref[r, (qb - 1) * blk:(qb + 1) * blk, lanes]
                vv = v_ref[r, (qb - 1) * blk:(qb + 1) * blk, lanes]
            q = q_ref[r, rows, lanes]
            zero = jnp.zeros_like(q)
            q2 = jnp.concatenate([jnp.where(low, q, zero), jnp.where(low, zero, q)], axis=0)
            s = _dot_nt(q2, kk)
            has_prev = (pl.program_id(1) * nqb + qb) > 0
            ss.append(jnp.where(in_window & (own_block | has_prev), s, -jnp.inf))
            vos.append(jnp.concatenate([vv, ones], axis=1))
        ms, ps = [], []
        for s in ss:
            m = jnp.max(s, axis=-1, keepdims=True)
            ps.append(jnp.exp(s - m).astype(BF16))
            ms.append(jnp.where(low, jnp.broadcast_to(m[:blk], (blk, LANES)),
                                jnp.broadcast_to(m[blk:], (blk, LANES))))
        for (r, qb, j), p, vo, m in zip(grp, ps, vos, ms):
            pv = _dot(p, vo)
            o = jnp.where(low, pv[:blk, :LANES], pv[blk:, :LANES])
            l = jnp.where(low, pv[:blk, LANES:], pv[blk:, LANES:])
            if dilation == 1:
                tok = slice(qb * blk, (qb + 1) * blk)
            else:
                tok = pl.ds(r + dilation * qb * blk, blk, stride=dilation)
            o_ref[j, tok, :] = o / l
            lse_ref[j, tok, :] = m + jnp.log(l)


def _window_attention(qkv, window, dilation):
    bsz, _, length, w3 = qkv.shape
    t = length * dilation
    width = w3 // 3
    heads = width // SW_HEAD_DIM
    span = window // dilation
    assert span <= SW_BLOCK and length % SW_BLOCK == 0
    nb = max(2, 16 // dilation)
    rl = nb * SW_BLOCK
    cur = lambda j: pl.BlockSpec((None, dilation, rl, width), lambda b, i: (b, 0, i, j))
    prev = lambda j: pl.BlockSpec((None, dilation, SW_BLOCK, width),
                                  lambda b, i: (b, 0, jnp.maximum(i * nb - 1, 0), j))
    out = pl.BlockSpec((None, width // LANES, rl * dilation, LANES), lambda b, i: (b, 0, i, 0))
    return pl.pallas_call(
        functools.partial(_sw_kernel, dilation=dilation, heads=heads, span=span),
        grid=(bsz, length // rl),
        in_specs=[cur(0), cur(1), cur(2), prev(1), prev(2)],
        out_specs=[out, out],
        out_shape=[jax.ShapeDtypeStruct((bsz, width // LANES, t, LANES), F32)] * 2,
        compiler_params=_params(2),
        name=f"window_attention_d{dilation}",
    )(qkv, qkv, qkv, qkv, qkv)


def _out_kernel(x_ref, ya_ref, yb_ref, o1_ref, l1_ref, o2_ref, l2_ref, o3_ref, l3_ref,
                gate1_ref, shift2_ref, scale2_ref, gate2_ref, fnw_ref,
                wout_ref, wfi_ref, wfo_ref, out_ref, *, ffn_chunks):
    ycs = []
    for j in range(o1_ref.shape[0]):
        l1, l2, l3 = l1_ref[j], l2_ref[j], l3_ref[j]
        m = jnp.maximum(jnp.maximum(l1, l2), l3)
        e1, e2, e3 = jnp.exp(l1 - m), jnp.exp(l2 - m), jnp.exp(l3 - m)
        yc = (e1 * o1_ref[j] + e2 * o2_ref[j] + e3 * o3_ref[j]) / (e1 + e2 + e3)
        ycs.append(yc.astype(BF16))
    y = jnp.concatenate([ya_ref[...].astype(BF16), yb_ref[...].astype(BF16)] + ycs, axis=1)
    x1 = x_ref[...] + gate1_ref[...] * _dot(y, wout_ref[...])
    ms = jnp.mean(x1 * x1, axis=-1, keepdims=True)
    h = x1 * lax.rsqrt(ms + NORM_EPS) * (fnw_ref[...] * (1.0 + scale2_ref[...])) + shift2_ref[...]
    hb = h.astype(BF16)
    hidden = wfo_ref.shape[0]
    fc = hidden // ffn_chunks
    acc = None
    for j in range(ffn_chunks):
        gate = _dot(hb, wfi_ref[:, j * fc:(j + 1) * fc])
        up = _dot(hb, wfi_ref[:, hidden + j * fc:hidden + (j + 1) * fc])
        part = _dot((_silu(gate) * up).astype(BF16), wfo_ref[j * fc:(j + 1) * fc, :])
        acc = part if acc is None else acc + part
    out_ref[...] = x1 + gate2_ref[...] * acc


def _out_ffn(x, ya, yb, attn, mod4, fnw, wout, wfi, wfo, layer, tm):
    bsz, t, d = x.shape
    row = lambda wd: pl.BlockSpec((None, tm, wd), lambda b, i: (b, i, 0))
    modspec = lambda k: pl.BlockSpec((None, None, 1, d), lambda b, i: (b, k, 0, 0))
    attn_flat = [a for pair in attn for a in pair]
    return pl.pallas_call(
        functools.partial(_out_kernel, ffn_chunks=1),
        grid=(bsz, t // tm),
        in_specs=[row(d), row(ya.shape[-1]), row(yb.shape[-1])]
        + [pl.BlockSpec((None, a.shape[1], tm, LANES), lambda b, i: (b, 0, i, 0)) for a in attn_flat]
        + [modspec(2), modspec(3), modspec(4), modspec(5), _resident(fnw.shape),
           _layer_resident(wout, layer), _layer_resident(wfi, layer),
           _layer_resident(wfo, layer)],
        out_specs=row(d),
        out_shape=jax.ShapeDtypeStruct((bsz, t, d), F32),
        compiler_params=_params(2),
        name="out_ffn",
    )(x, ya, yb, *attn_flat, mod4, mod4, mod4, mod4, fnw, wout, wfi, wfo)


def _rotary_tables(t, heads):
    half = ROPE_DIM // 2
    inv = ROPE_THETA ** (-jnp.arange(0, ROPE_DIM, 2, dtype=F32) / ROPE_DIM)
    ang = jnp.arange(t, dtype=F32)[:, None] * inv[None, :]
    cos, sin = jnp.cos(ang), jnp.sin(ang)
    rest = SW_HEAD_DIM - ROPE_DIM
    ones = jnp.ones((t, rest), F32)
    zeros = jnp.zeros((t, rest), F32)
    zh = jnp.zeros((t, half), F32)
    cos_f = jnp.concatenate([cos, cos, ones], axis=1)
    sin_a = jnp.concatenate([-sin, zh, zeros], axis=1)
    sin_b = jnp.concatenate([zh, sin, zeros], axis=1)
    tile = lambda a: jnp.tile(a, (1, heads))
    return tile(cos_f), tile(sin_a), tile(sin_b)


def kernel(x, c, w_mod, b_mod, mix_norm_w, ffn_norm_w, w_in, w_out, dn_conv_w, dn_a_log,
           dn_dt_bias, dn_out_norm_w, gm_ln_g, gm_ln_b, gm_w_s, gm_b_s, sw_q_norm_w,
           sw_k_norm_w, w_ffn_in, w_ffn_out):
    bsz, t, d = x.shape
    nl = w_mod.shape[0]
    dn_heads = dn_a_log.shape[1]
    dn_width = dn_heads * DN_HEAD_DIM
    gm_groups = gm_w_s.shape[1]
    gm_width = gm_groups * GM_GROUP_DIM
    npat = len(SW_PATTERNS)
    sw_width = (w_in.shape[2] - 4 * dn_width - 2 * dn_heads - 2 * gm_width) // (3 * npat)
    sw_heads = sw_width // SW_HEAD_DIM
    assert 2 * dn_heads <= AB_PAD

    o_z = 3 * dn_width
    o_a = o_z + dn_width
    o_gm = o_a + 2 * dn_heads
    o_sw = o_gm + 2 * gm_width
    widths = (3 * dn_width, dn_width, 2 * gm_width) + (3 * sw_width,) * npat + (AB_PAD,)

    mod = _modulation(c, w_mod, b_mod)
    tables = _rotary_tables(t, sw_heads)
    ii = jnp.arange(sw_width) // SW_HEAD_DIM
    ones_bd = ((ii[:, None] == ii[None, :]) * (1.0 / SW_HEAD_DIM)).astype(BF16)
    tm = 512
    ab_cols = jnp.zeros((nl, d, AB_PAD), F32).at[:, :, :2 * dn_heads].set(w_in[:, :, o_a:o_gm])
    w_cat = jnp.concatenate(
        [w_in[:, :, :o_a], w_in[:, :, o_gm:o_sw], w_in[:, :, o_sw:], ab_cols], axis=2).astype(BF16)
    w_out_b, w_ffn_in_b, w_ffn_out_b = (w.astype(BF16) for w in (w_out, w_ffn_in, w_ffn_out))

    for layer in range(nl):
        mod4 = mod[layer].reshape(bsz, 6, 1, d)
        dn_qkv, dn_z, gm_uv, sw1, sw2, sw3, dn_ab = _in_projection(
            x, mod4, mix_norm_w[layer].reshape(1, d), w_cat, layer, tables,
            jnp.tile(sw_q_norm_w[layer] * (SW_HEAD_DIM ** -0.5), sw_heads).reshape(1, sw_width),
            jnp.tile(sw_k_norm_w[layer], sw_heads).reshape(1, sw_width),
            ones_bd, dn_conv_w[layer], widths, tm)

        pad_heads = lambda v: jnp.zeros((1, AB_PAD), F32).at[0, :dn_heads].set(v)
        y_a = _deltanet(dn_qkv, dn_z, dn_ab, pad_heads(dn_a_log[layer]),
                        pad_heads(dn_dt_bias[layer]), dn_out_norm_w[layer].reshape(1, DN_HEAD_DIM),
                        tt=128, nseq=8 if bsz % 8 == 0 else 1)
        bias_full = jnp.repeat(gm_b_s[layer].T, GM_GROUP_DIM, axis=1)
        y_b = _spatial_gating(gm_uv, gm_ln_g[layer].reshape(1, gm_width),
                              gm_ln_b[layer].reshape(1, gm_width), gm_w_s[layer], bias_full,
                              tt=1024)
        attn = [_window_attention(sw, window, dilation)
                for sw, (window, dilation) in zip((sw1, sw2, sw3), SW_PATTERNS)]
        x = _out_ffn(x, y_a, y_b, attn, mod4, ffn_norm_w[layer].reshape(1, d),
                     w_out_b, w_ffn_in_b, w_ffn_out_b, layer, tm)
    return x
```
